```python
import jax
import jax.numpy as jnp
from jax import lax
import numpy as np

D_MODEL = 1024
BATCH = 8
SEQ = 4096
DEPTH = 4

CHUNK = 64
D_PLE = 256
D_FF = 4 * D_MODEL
NORM_EPS = 1e-6

SSD_HEADS = 8
SSD_HEAD_DIM = 64
SSD_WIDTH = SSD_HEADS * SSD_HEAD_DIM
SSD_GROUPS = 2
SSD_STATE = 128
SSD_CONV = 4
SSD_XBC = SSD_WIDTH + 2 * SSD_GROUPS * SSD_STATE

GLA_HEADS = 4
GLA_HEAD_DIM = 64
GLA_WIDTH = GLA_HEADS * GLA_HEAD_DIM
GLA_GATE_RANK = 16
GLA_GATE_NORMALIZER = 16.0

RWKV_HEADS = 4
RWKV_HEAD_DIM = 64
RWKV_WIDTH = RWKV_HEADS * RWKV_HEAD_DIM
RWKV_DECAY_RANK = 32
RWKV_ICLR_RANK = 32
RWKV_VRES_RANK = 16
RWKV_GATE_RANK = 64
RWKV_LNX_EPS = 64e-5

MIX_WIDTH = SSD_WIDTH + GLA_WIDTH + RWKV_WIDTH
SSD_COLS = SSD_WIDTH + SSD_XBC + SSD_HEADS
GLA_COLS = 4 * GLA_WIDTH + GLA_GATE_RANK
RWKV_COLS = 3 * RWKV_WIDTH + RWKV_DECAY_RANK + RWKV_ICLR_RANK + RWKV_GATE_RANK
IN_COLS = SSD_COLS + GLA_COLS + RWKV_COLS

kernel_name = 'hybrid_ssd_gla_rwkv7_stream_block'


def split_cols(a, sizes):
    idx = [int(i) for i in np.cumsum(sizes)[:-1]]
    return jnp.split(a, idx, axis=-1)


def rms_norm(x, g, eps=NORM_EPS):
    xf = x.astype(jnp.float32)
    y = xf * lax.rsqrt(jnp.mean(xf * xf, axis=-1, keepdims=True) + eps)
    return (y * g.astype(jnp.float32)).astype(x.dtype)


def token_shift(u):
    return jnp.pad(u, ((0, 0), (1, 0), (0, 0)))[:, :-1]


def causal_dwconv(u, w, b):
    k = w.shape[0]
    out = lax.conv_general_dilated(u, w[:, None, :].astype(u.dtype), window_strides=(1,),
                                   padding=[(k - 1, 0)], dimension_numbers=('NWC', 'WIO', 'NWC'),
                                   feature_group_count=u.shape[-1])
    return out + b


def scan_chunk_states(decay, inc):
    def step(s, di):
        d, i = di
        return (d * s + i).astype(s.dtype), s
    _, prev = lax.scan(step, jnp.zeros(inc.shape[1:], inc.dtype), (decay, inc))
    return prev


def ssd_mixer(z, xbc, dt, conv_w, conv_b, dt_bias, a_log, d_skip, norm_g):
    bsz, t, _ = z.shape
    nc = t // CHUNK
    hg = SSD_HEADS // SSD_GROUPS
    xbc = jax.nn.silu(causal_dwconv(xbc, conv_w, conv_b))
    xs, bm, cm = split_cols(xbc, [SSD_WIDTH, SSD_GROUPS * SSD_STATE, SSD_GROUPS * SSD_STATE])
    dt = jax.nn.softplus(dt + dt_bias)
    a = -jnp.exp(a_log)
    xh = xs.reshape(bsz, nc, CHUNK, SSD_GROUPS, hg, SSD_HEAD_DIM)
    bm = bm.reshape(bsz, nc, CHUNK, SSD_GROUPS, SSD_STATE)
    cm = cm.reshape(bsz, nc, CHUNK, SSD_GROUPS, SSD_STATE)
    dtc = dt.reshape(bsz, nc, CHUNK, SSD_GROUPS, hg)
    cs = jnp.cumsum(dtc * a.reshape(SSD_GROUPS, hg), axis=2)
    xdt = xh * dtc[..., None]
    causal = jnp.tril(jnp.ones((CHUNK, CHUNK), dtype=bool))[:, :, None, None]
    seg = cs[:, :, :, None] - cs[:, :, None, :]
    lmat = jnp.exp(jnp.where(causal, seg, -jnp.inf))
    cb = jnp.einsum('bclgn,bcsgn->bclsg', cm, bm)
    y_diag = jnp.einsum('bclsgh,bcsghp->bclghp', cb[..., None] * lmat, xdt)
    xw = xdt * jnp.exp(cs[:, :, -1:] - cs)[..., None]
    inc = jnp.einsum('bclgn,bclghp->bcghpn', bm, xw)
    chunk_decay = jnp.exp(cs[:, :, -1])
    prev = scan_chunk_states(jnp.moveaxis(chunk_decay, 1, 0)[..., None, None], jnp.moveaxis(inc, 1, 0))
    prev = jnp.moveaxis(prev, 0, 1)
    y_off = jnp.einsum('bclgn,bcghpn->bclghp', cm, prev) * jnp.exp(cs)[..., None]
    y = (y_diag + y_off).reshape(bsz, t, SSD_HEADS, SSD_HEAD_DIM)
    y = y + xs.reshape(bsz, t, SSD_HEADS, SSD_HEAD_DIM) * d_skip[:, None]
    y = y.reshape(bsz, t, SSD_WIDTH) * jax.nn.silu(z)
    gw = SSD_WIDTH // SSD_GROUPS
    y = rms_norm(y.reshape(bsz, t, SSD_GROUPS, gw), norm_g.reshape(SSD_GROUPS, gw))
    return y.reshape(bsz, t, SSD_WIDTH)


def gla_mixer(q, k, v, g, alpha_lr, alpha_w, alpha_b, norm_g):
    bsz, t, _ = q.shape
    nc = t // CHUNK
    shp = (bsz, nc, CHUNK, GLA_HEADS, GLA_HEAD_DIM)
    log_alpha = jax.nn.log_sigmoid(alpha_lr @ alpha_w + alpha_b) / GLA_GATE_NORMALIZER
    q = q.reshape(shp) * (GLA_HEAD_DIM ** -0.5)
    k = k.reshape(shp)
    v = v.reshape(shp)
    bcum = jnp.cumsum(log_alpha.reshape(shp), axis=2)
    q_in = q * jnp.exp(bcum)
    k_in = k * jnp.exp(-bcum)
    causal = jnp.tril(jnp.ones((CHUNK, CHUNK), dtype=bool))
    scores = jnp.where(causal, jnp.einsum('bclhd,bcshd->bchls', q_in, k_in), 0.0)
    o_intra = jnp.einsum('bchls,bcshd->bclhd', scores, v)
    k_end = k * jnp.exp(bcum[:, :, -1:] - bcum)
    inc = jnp.einsum('bclhk,bclhv->bchkv', k_end, v)
    dec = jnp.exp(bcum[:, :, -1])
    prev = scan_chunk_states(jnp.moveaxis(dec, 1, 0)[..., None], jnp.moveaxis(inc, 1, 0))
    prev = jnp.moveaxis(prev, 0, 1)
    o_inter = jnp.einsum('bclhk,bchkv->bclhv', q_in, prev)
    o = rms_norm(o_intra + o_inter, norm_g) * jax.nn.silu(g.reshape(shp))
    return o.reshape(bsz, t, GLA_WIDTH)


def rwkv7_mixer(cols, mu, w0, w2, a0, a2, g2, k_k, k_a, r_k, lnx_g, lnx_b, v_first, vres):
    bsz, t, _ = cols.shape
    cols = cols + (token_shift(cols) - cols) * mu
    r, k, v, w_lr, a_lr, g_lr = split_cols(cols, [RWKV_WIDTH] * 3 + [RWKV_DECAY_RANK, RWKV_ICLR_RANK, RWKV_GATE_RANK])
    log_w = -jax.nn.softplus(-(w0 + jnp.tanh(w_lr) @ w2)) - 0.5
    decay = jnp.exp(-jnp.exp(log_w))
    iclr = jax.nn.sigmoid(a0 + a_lr @ a2)
    gate = jax.nn.sigmoid(g_lr) @ g2
    if vres is None:
        v_first = v
    else:
        v0, v1, v2 = vres
        v = v + (v_first - v) * jax.nn.sigmoid(v0 + (v @ v1) @ v2)

    def heads(u):
        return u.reshape(bsz, t, RWKV_HEADS, RWKV_HEAD_DIM)

    kk = heads(k * k_k)
    kkf = kk.astype(jnp.float32)
    kk = (kkf / jnp.maximum(jnp.sqrt(jnp.sum(kkf * kkf, axis=-1, keepdims=True)), 1e-12)).astype(k.dtype)
    k = k * (1.0 + (iclr - 1.0) * k_a)
    rh, wh, kh, vh = heads(r), heads(decay), heads(k), heads(v)
    ah, bh = -kk, kk * heads(iclr)

    def step(s, inp):
        r_t, w_t, k_t, v_t, a_t, b_t = inp
        sa = jnp.einsum('bhij,bhj->bhi', s, a_t)
        s = s * w_t[:, :, None, :] + sa[..., None] * b_t[:, :, None, :] + v_t[..., None] * k_t[:, :, None, :]
        return s, jnp.einsum('bhij,bhj->bhi', s, r_t)

    xs = tuple(jnp.moveaxis(u, 1, 0) for u in (rh, wh, kh, vh, ah, bh))
    s0 = jnp.zeros((bsz, RWKV_HEADS, RWKV_HEAD_DIM, RWKV_HEAD_DIM), r.dtype)
    _, ys = lax.scan(step, s0, xs)
    y = jnp.moveaxis(ys, 0, 1).astype(jnp.float32)
    mean = jnp.mean(y, axis=-1, keepdims=True)
    var = jnp.mean(jnp.square(y - mean), axis=-1, keepdims=True)
    y = ((y - mean) * lax.rsqrt(var + RWKV_LNX_EPS)).astype(r.dtype)
    y = y * lnx_g.reshape(RWKV_HEADS, RWKV_HEAD_DIM) + lnx_b.reshape(RWKV_HEADS, RWKV_HEAD_DIM)
    y = y + jnp.sum(rh * kh * r_k, axis=-1, keepdims=True) * vh
    return y.reshape(bsz, t, RWKV_WIDTH) * gate, v_first


def setup_inputs(seed: int = 0) -> dict:
    key = jax.random.key(seed)
    keys = jax.random.split(key, 48)
    counter = iter(range(48))

    def nrm(shape, scale):
        return jax.random.normal(keys[next(counter)], shape, jnp.float32) * scale

    def unif(shape, lo, hi):
        return jax.random.uniform(keys[next(counter)], shape, jnp.float32, minval=lo, maxval=hi)

    def gain(shape):
        return 1.0 + nrm(shape, 0.02)

    x = nrm((BATCH, SEQ, D_MODEL), 1.0)
    p = nrm((DEPTH, BATCH, SEQ, D_PLE), 1.0)
    norm_mix_g = gain((DEPTH, D_MODEL))
    w_in = nrm((DEPTH, D_MODEL, IN_COLS), D_MODEL ** -0.5)
    ssd_conv_w = nrm((DEPTH, SSD_CONV, SSD_XBC), SSD_CONV ** -0.5)
    ssd_conv_b = nrm((DEPTH, SSD_XBC), 0.02)
    dt0 = jnp.exp(unif((DEPTH, SSD_HEADS), float(np.log(1e-3)), float(np.log(1e-1))))
    ssd_dt_bias = dt0 + jnp.log(-jnp.expm1(-dt0))
    ssd_a_log = jnp.log(unif((DEPTH, SSD_HEADS), 1.0, 16.0))
    ssd_d = 1.0 + nrm((DEPTH, SSD_HEADS), 0.1)
    ssd_norm_g = gain((DEPTH, SSD_WIDTH))
    gla_alpha_w = nrm((DEPTH, GLA_GATE_RANK, GLA_WIDTH), GLA_GATE_RANK ** -0.5)
    gla_alpha_b = nrm((DEPTH, GLA_WIDTH), 0.1)
    gla_norm_g = gain((DEPTH, GLA_HEAD_DIM))
    rwkv_mu = unif((DEPTH, RWKV_COLS), 0.0, 1.0)
    rwkv_w0 = unif((DEPTH, RWKV_WIDTH), -5.0, -1.0)
    rwkv_w2 = nrm((DEPTH, RWKV_DECAY_RANK, RWKV_WIDTH), 0.5 * RWKV_DECAY_RANK ** -0.5)
    rwkv_a0 = nrm((DEPTH, RWKV_WIDTH), 0.1)
    rwkv_a2 = nrm((DEPTH, RWKV_ICLR_RANK, RWKV_WIDTH), RWKV_ICLR_RANK ** -0.5)
    rwkv_g2 = nrm((DEPTH, RWKV_GATE_RANK, RWKV_WIDTH), RWKV_GATE_RANK ** -0.5)
    rwkv_k_k = 1.0 + nrm((DEPTH, RWKV_WIDTH), 0.1)
    rwkv_k_a = 1.0 + nrm((DEPTH, RWKV_WIDTH), 0.1)
    rwkv_r_k = nrm((DEPTH, RWKV_HEADS, RWKV_HEAD_DIM), 0.1)
    rwkv_lnx_g = gain((DEPTH, RWKV_WIDTH))
    rwkv_lnx_b = nrm((DEPTH, RWKV_WIDTH), 0.02)
    rwkv_v0 = nrm((DEPTH - 1, RWKV_WIDTH), 0.1)
    rwkv_v1 = nrm((DEPTH - 1, RWKV_WIDTH, RWKV_VRES_RANK), RWKV_WIDTH ** -0.5)
    rwkv_v2 = nrm((DEPTH - 1, RWKV_VRES_RANK, RWKV_WIDTH), RWKV_VRES_RANK ** -0.5)
    w_out = nrm((DEPTH, MIX_WIDTH, D_MODEL), MIX_WIDTH ** -0.5)
    norm_mlp_g = gain((DEPTH, D_MODEL))
    w_up = nrm((DEPTH, D_MODEL, D_FF), D_MODEL ** -0.5)
    w_down = nrm((DEPTH, D_FF, D_MODEL), D_FF ** -0.5)
    norm_ple_g = gain((DEPTH, D_MODEL))
    w_ple_gate = nrm((DEPTH, D_MODEL, D_MODEL), D_MODEL ** -0.5)
    w_ple_proj = nrm((DEPTH, D_PLE, D_MODEL), D_PLE ** -0.5)
    norm_final_g = gain((D_MODEL,))
    return {'x': x, 'p': p, 'norm_mix_g': norm_mix_g, 'w_in': w_in,
            'ssd_conv_w': ssd_conv_w, 'ssd_conv_b': ssd_conv_b, 'ssd_dt_bias': ssd_dt_bias,
            'ssd_a_log': ssd_a_log, 'ssd_d': ssd_d, 'ssd_norm_g': ssd_norm_g,
            'gla_alpha_w': gla_alpha_w, 'gla_alpha_b': gla_alpha_b, 'gla_norm_g': gla_norm_g,
            'rwkv_mu': rwkv_mu, 'rwkv_w0': rwkv_w0, 'rwkv_w2': rwkv_w2, 'rwkv_a0': rwkv_a0,
            'rwkv_a2': rwkv_a2, 'rwkv_g2': rwkv_g2, 'rwkv_k_k': rwkv_k_k, 'rwkv_k_a': rwkv_k_a,
            'rwkv_r_k': rwkv_r_k, 'rwkv_lnx_g': rwkv_lnx_g, 'rwkv_lnx_b': rwkv_lnx_b,
            'rwkv_v0': rwkv_v0, 'rwkv_v1': rwkv_v1, 'rwkv_v2': rwkv_v2, 'w_out': w_out,
            'norm_mlp_g': norm_mlp_g, 'w_up': w_up, 'w_down': w_down, 'norm_ple_g': norm_ple_g,
            'w_ple_gate': w_ple_gate, 'w_ple_proj': w_ple_proj, 'norm_final_g': norm_final_g}


def reference(x, p, norm_mix_g, w_in, ssd_conv_w, ssd_conv_b, ssd_dt_bias, ssd_a_log, ssd_d,
              ssd_norm_g, gla_alpha_w, gla_alpha_b, gla_norm_g, rwkv_mu, rwkv_w0, rwkv_w2,
              rwkv_a0, rwkv_a2, rwkv_g2, rwkv_k_k, rwkv_k_a, rwkv_r_k, rwkv_lnx_g, rwkv_lnx_b,
              rwkv_v0, rwkv_v1, rwkv_v2, w_out, norm_mlp_g, w_up, w_down, norm_ple_g,
              w_ple_gate, w_ple_proj, norm_final_g):
    h = x
    v_first = None
    for i in range(DEPTH):
        xn = rms_norm(h, norm_mix_g[i])
        cols = xn @ w_in[i]
        ssd_c, gla_c, rwkv_c = split_cols(cols, [SSD_COLS, GLA_COLS, RWKV_COLS])
        z, xbc, dt = split_cols(ssd_c, [SSD_WIDTH, SSD_XBC, SSD_HEADS])
        y_ssd = ssd_mixer(z, xbc, dt, ssd_conv_w[i], ssd_conv_b[i], ssd_dt_bias[i],
                          ssd_a_log[i], ssd_d[i], ssd_norm_g[i])
        gq, gk, gv, gg, g_lr = split_cols(gla_c, [GLA_WIDTH] * 4 + [GLA_GATE_RANK])
        y_gla = gla_mixer(gq, gk, gv, gg, g_lr, gla_alpha_w[i], gla_alpha_b[i], gla_norm_g[i])
        vres = None if i == 0 else (rwkv_v0[i - 1], rwkv_v1[i - 1], rwkv_v2[i - 1])
        y_rwkv, v_first = rwkv7_mixer(rwkv_c, rwkv_mu[i], rwkv_w0[i], rwkv_w2[i], rwkv_a0[i],
                                      rwkv_a2[i], rwkv_g2[i], rwkv_k_k[i], rwkv_k_a[i],
                                      rwkv_r_k[i], rwkv_lnx_g[i], rwkv_lnx_b[i], v_first, vres)
        h = h + jnp.concatenate([y_ssd, y_gla, y_rwkv], axis=-1) @ w_out[i]
        hn = rms_norm(h, norm_mlp_g[i])
        h = h + jnp.square(jax.nn.relu(hn @ w_up[i])) @ w_down[i]
        gate = jax.nn.sigmoid(rms_norm(h, norm_ple_g[i]) @ w_ple_gate[i])
        h = h + gate * (p[i] @ w_ple_proj[i])
    return rms_norm(h, norm_final_g)
```

```python
import functools

import jax
import jax.numpy as jnp
from jax import lax
from jax.experimental import pallas as pl
from jax.experimental.pallas import tpu as pltpu

F32 = jnp.float32
BF16 = jnp.bfloat16

NORM_EPS = 1e-6
HEAD_DIM = 64
SSD_HEADS = 8
SSD_GROUPS = 2
SSD_STATE = 128
SSD_WIDTH = SSD_HEADS * HEAD_DIM
SSD_XBC = SSD_WIDTH + 2 * SSD_GROUPS * SSD_STATE
SSD_CONV = 4
GLA_WIDTH = 4 * HEAD_DIM
GLA_GATE_RANK = 16
GLA_GATE_NORMALIZER = 16.0
RWKV_WIDTH = 4 * HEAD_DIM
RWKV_DECAY_RANK = 32
RWKV_ICLR_RANK = 32
RWKV_GATE_RANK = 64
RWKV_LR = RWKV_DECAY_RANK + RWKV_ICLR_RANK + RWKV_GATE_RANK
RWKV_COLS = 3 * RWKV_WIDTH + RWKV_LR
RWKV_LNX_EPS = 64e-5
SMALL = 128

CHUNK = 64
ROW_BLOCK = 256
VMEM_LIMIT = 56 * 1024 * 1024


def _dot(a, b):
    return jnp.dot(a.astype(BF16), b.astype(BF16), preferred_element_type=F32)


def _dot_nt(a, b):
    return lax.dot_general(a.astype(BF16), b.astype(BF16), (((1,), (1,)), ((), ())),
                           preferred_element_type=F32)


def _dot_tn(a, b):
    return lax.dot_general(a.astype(BF16), b.astype(BF16), (((0,), (0,)), ((), ())),
                           preferred_element_type=F32)


def _split(a, pieces):
    out = []
    r = a
    for _ in range(pieces):
        p = r.astype(BF16)
        out.append(p)
        r = r - p.astype(F32)
    return out


def _sel_dot(m01, a, pieces=3):
    acc = None
    for p in _split(a, pieces):
        t = jnp.dot(m01, p, preferred_element_type=F32)
        acc = t if acc is None else acc + t
    return acc


def _dot_sel(a, m01, pieces=3):
    acc = None
    for p in _split(a, pieces):
        t = jnp.dot(p, m01, preferred_element_type=F32)
        acc = t if acc is None else acc + t
    return acc


def _dot2(a, b):
    ah, al = _split(a, 2)
    bh, bl = _split(b, 2)
    return (jnp.dot(ah, bh, preferred_element_type=F32)
            + jnp.dot(ah, bl, preferred_element_type=F32)
            + jnp.dot(al, bh, preferred_element_type=F32))


def _iota2(n, m):
    return (lax.broadcasted_iota(jnp.int32, (n, m), 0),
            lax.broadcasted_iota(jnp.int32, (n, m), 1))


def _softplus(x):
    return jnp.maximum(x, 0.0) + jnp.log1p(jnp.exp(-jnp.abs(x)))


def _sigmoid(x):
    return 1.0 / (1.0 + jnp.exp(-x))


def _rms(x, g):
    return x * lax.rsqrt(jnp.mean(x * x, axis=-1, keepdims=True) + NORM_EPS) * g


def _in_proj_kernel(h_ref, g_ref, w_ref, ssd_ref, gla_ref, rwkv_ref, small_ref):
    xn = _rms(h_ref[...], g_ref[...]).astype(BF16)
    o = 0
    for ref in (ssd_ref, gla_ref, rwkv_ref, small_ref):
        n = ref.shape[-1]
        ref[...] = jnp.dot(xn, w_ref[:, o:o + n], preferred_element_type=F32)
        o += n


def _in_proj(h, g, w, layer):
    b, t, d = h.shape
    tm = ROW_BLOCK
    widths = (SSD_WIDTH + SSD_XBC, 4 * GLA_WIDTH, RWKV_COLS, SMALL)
    total = sum(widths)
    row = lambda bi, ti: (bi, ti, 0)
    return pl.pallas_call(
        _in_proj_kernel,
        grid=(b, t // tm),
        in_specs=[
            pl.BlockSpec((None, tm, d), row),
            pl.BlockSpec((None, 1, d), lambda bi, ti: (layer, 0, 0)),
            pl.BlockSpec((None, d, total), lambda bi, ti: (layer, 0, 0)),
        ],
        out_specs=[pl.BlockSpec((None, tm, n), row) for n in widths],
        out_shape=[jax.ShapeDtypeStruct((b, t, n), F32) for n in widths],
        compiler_params=pltpu.CompilerParams(
            dimension_semantics=("parallel", "parallel"), vmem_limit_bytes=VMEM_LIMIT),
        name="in_proj",
    )(h, g, w)


def _ssd_kernel(main_ref, small_ref, cw_ref, cb_ref, dtb_ref, aneg_ref, dskip_ref, ng_ref,
                y_ref, tail_ref, state_ref):
    L = main_ref.shape[0]
    hg = SSD_HEADS // SSD_GROUPS

    @pl.when(pl.program_id(1) == 0)
    def _():
        tail_ref[...] = jnp.zeros_like(tail_ref)
        state_ref[...] = jnp.zeros_like(state_ref)

    z = main_ref[:, 0:SSD_WIDTH]
    u = main_ref[:, SSD_WIDTH:SSD_WIDTH + SSD_XBC]
    full = jnp.concatenate([tail_ref[...], u], axis=0)
    acc = u * cw_ref[SSD_CONV - 1:SSD_CONV, :] + cb_ref[...]
    for j in range(1, SSD_CONV):
        acc = acc + pltpu.roll(full, j, axis=0)[8:, :] * cw_ref[SSD_CONV - 1 - j:SSD_CONV - j, :]
    tail_ref[...] = u[L - 8:, :]
    xbc = acc * _sigmoid(acc)
    xs = xbc[:, 0:SSD_WIDTH]
    bm = xbc[:, SSD_WIDTH:SSD_WIDTH + SSD_GROUPS * SSD_STATE]
    cm = xbc[:, SSD_WIDTH + SSD_GROUPS * SSD_STATE:]

    dt = _softplus(small_ref[...] + dtb_ref[...])
    da = dt * aneg_ref[...]
    ri, ci = _iota2(L, L)
    causal = ri >= ci
    tril = jnp.where(causal, 1.0, 0.0).astype(BF16)
    cs = _sel_dot(tril, da)
    cs_t = cs.T
    cs_last = cs[L - 1:L, :]
    e_cs = jnp.exp(cs)
    e_rem = jnp.exp(cs_last - cs)
    e_last = jnp.exp(cs_last)

    ys = []
    for g in range(SSD_GROUPS):
        bg_t = bm[:, g * SSD_STATE:(g + 1) * SSD_STATE].T
        cg = cm[:, g * SSD_STATE:(g + 1) * SSD_STATE]
        cb = _dot(cg, bg_t)
        st = state_ref[g]
        y_off = _dot(cg, st)
        xws, decs = [], []
        for hh in range(hg):
            h = g * hg + hh
            seg = cs[:, h:h + 1] - cs_t[h:h + 1, :]
            lmat = jnp.exp(jnp.where(causal, seg, -jnp.inf))
            xdt = xs[:, h * HEAD_DIM:(h + 1) * HEAD_DIM] * dt[:, h:h + 1]
            y_h = _dot(cb * lmat, xdt) + y_off[:, hh * HEAD_DIM:(hh + 1) * HEAD_DIM] * e_cs[:, h:h + 1]
            ys.append(y_h)
            xws.append(xdt * e_rem[:, h:h + 1])
            decs.append(jnp.broadcast_to(e_last[:, h:h + 1], (1, HEAD_DIM)))
        inc = _dot(bg_t, jnp.concatenate(xws, axis=1))
        state_ref[g] = st * jnp.concatenate(decs, axis=1) + inc

    y = jnp.concatenate(ys, axis=1) + xs * dskip_ref[...]
    y = y * (z * _sigmoid(z))
    gw = SSD_WIDTH // SSD_GROUPS
    for g in range(SSD_GROUPS):
        sl = slice(g * gw, (g + 1) * gw)
        y_ref[:, sl] = _rms(y[:, sl], ng_ref[:, sl])


def _ssd(main, small, cw, cb, dtb, aneg, dskip, ng, layer):
    b, t, _ = main.shape
    L = ROW_BLOCK
    row = lambda bi, ti: (bi, ti, 0)
    par = lambda bi, ti: (layer, 0, 0)
    return pl.pallas_call(
        _ssd_kernel,
        grid=(b, t // L),
        in_specs=[
            pl.BlockSpec((None, L, main.shape[-1]), row),
            pl.BlockSpec((None, L, SMALL), row),
            pl.BlockSpec((None, SSD_CONV, SSD_XBC), par),
            pl.BlockSpec((None, 1, SSD_XBC), par),
            pl.BlockSpec((None, 1, SMALL), par),
            pl.BlockSpec((None, 1, SMALL), par),
            pl.BlockSpec((None, 1, SSD_WIDTH), par),
            pl.BlockSpec((None, 1, SSD_WIDTH), par),
        ],
        out_specs=pl.BlockSpec((None, L, SSD_WIDTH), row),
        out_shape=jax.ShapeDtypeStruct((b, t, SSD_WIDTH), F32),
        scratch_shapes=[
            pltpu.VMEM((8, SSD_XBC), F32),
            pltpu.VMEM((SSD_GROUPS, SSD_STATE, SSD_WIDTH // SSD_GROUPS), F32),
        ],
        compiler_params=pltpu.CompilerParams(
            dimension_semantics=("parallel", "arbitrary"), vmem_limit_bytes=VMEM_LIMIT),
        name="ssd_mixer",
    )(main, small, cw, cb, dtb, aneg, dskip, ng)


def _chunk_masks(L):
    ri, ci = _iota2(L, L)
    same = (ri // CHUNK) == (ci // CHUNK)
    return same, same & (ri >= ci), same & (ri > ci)


def _gla_kernel(main_ref, small_ref, aw_ref, ab_ref, ng_ref, y_ref, state_ref):
    L = main_ref.shape[0]
    nh = GLA_WIDTH // HEAD_DIM
    W = GLA_WIDTH

    @pl.when(pl.program_id(1) == 0)
    def _():
        state_ref[...] = jnp.zeros_like(state_ref)

    q = main_ref[:, 0:W] * (HEAD_DIM ** -0.5)
    k = main_ref[:, W:2 * W]
    v = main_ref[:, 2 * W:3 * W]
    gg = main_ref[:, 3 * W:4 * W]
    x = _dot(small_ref[...], aw_ref[...]) + ab_ref[...]
    la = -_softplus(-x) * (1.0 / GLA_GATE_NORMALIZER)
    same, incl, _ = _chunk_masks(L)
    bcum = _sel_dot(jnp.where(incl, 1.0, 0.0).astype(BF16), la)
    btot = _sel_dot(jnp.where(same, 1.0, 0.0).astype(BF16), la)
    q_in = q * jnp.exp(bcum)
    k_in = k * jnp.exp(-bcum)
    k_end = k * jnp.exp(btot - bcum)
    dec = jnp.exp(btot)

    heads = []
    for h in range(nh):
        hs = slice(h * HEAD_DIM, (h + 1) * HEAD_DIM)
        scores = jnp.where(incl, _dot_nt(q_in[:, hs], k_in[:, hs]), 0.0)
        o_intra = _dot(scores, v[:, hs])
        st = state_ref[h]
        parts = []
        for c in range(L // CHUNK):
            rs = slice(c * CHUNK, (c + 1) * CHUNK)
            parts.append(_dot_nt(q_in[rs, hs], st))
            st = st * dec[c * CHUNK:c * CHUNK + 1, hs] + _dot_tn(v[rs, hs], k_end[rs, hs])
        state_ref[h] = st
        heads.append(o_intra + jnp.concatenate(parts, axis=0))
    o = jnp.concatenate(heads, axis=1)

    ri, ci = _iota2(W, W)
    seg = jnp.where((ri // HEAD_DIM) == (ci // HEAD_DIM), 1.0, 0.0).astype(BF16)
    ms = _dot_sel(o * o, seg) * (1.0 / HEAD_DIM)
    o = o * lax.rsqrt(ms + NORM_EPS) * ng_ref[...]
    y_ref[...] = o * (gg * _sigmoid(gg))


def _gla(main, small, aw, ab, ng, layer):
    b, t, _ = main.shape
    L = ROW_BLOCK
    row = lambda bi, ti: (bi, ti, 0)
    par = lambda bi, ti: (layer, 0, 0)
    return pl.pallas_call(
        _gla_kernel,
        grid=(b, t // L),
        in_specs=[
            pl.BlockSpec((None, L, 4 * GLA_WIDTH), row),
            pl.BlockSpec((None, L, SMALL), row),
            pl.BlockSpec((None, SMALL, GLA_WIDTH), par),
            pl.BlockSpec((None, 1, GLA_WIDTH), par),
            pl.BlockSpec((None, 1, GLA_WIDTH), par),
        ],
        out_specs=pl.BlockSpec((None, L, GLA_WIDTH), row),
        out_shape=jax.ShapeDtypeStruct((b, t, GLA_WIDTH), F32),
        scratch_shapes=[pltpu.VMEM((GLA_WIDTH // HEAD_DIM, HEAD_DIM, HEAD_DIM), F32)],
        compiler_params=pltpu.CompilerParams(
            dimension_semantics=("parallel", "arbitrary"), vmem_limit_bytes=VMEM_LIMIT),
        name="gla_mixer",
    )(main, small, aw, ab, ng)


def _rwkv_kernel(first, cols_ref, vf_ref, mu_ref, w0_ref, w2_ref, a0_ref, a2_ref, g2_ref,
                 kk_ref, ka_ref, rk_ref, lg_ref, lb_ref, v0_ref, v1_ref, v2_ref,
                 y_ref, vout_ref, prev_ref, state_ref):
    L = cols_ref.shape[0]
    W = RWKV_WIDTH
    nh = W // HEAD_DIM

    @pl.when(pl.program_id(1) == 0)
    def _():
        prev_ref[...] = jnp.zeros_like(prev_ref)
        state_ref[...] = jnp.zeros_like(state_ref)

    cols = cols_ref[...]
    full = jnp.concatenate([prev_ref[...], cols], axis=0)
    shifted = pltpu.roll(full, 1, axis=0)[8:, :]
    prev_ref[...] = cols[L - 8:, :]
    xx = cols + (shifted - cols) * mu_ref[...]
    r = xx[:, 0:W]
    k = xx[:, W:2 * W]
    v = xx[:, 2 * W:3 * W]
    lr = xx[:, 3 * W:]
    log_w = -_softplus(-(w0_ref[...] + _dot(jnp.tanh(lr), w2_ref[...]))) - 0.5
    lw = -jnp.exp(log_w)
    iclr = _sigmoid(a0_ref[...] + _dot(lr, a2_ref[...]))
    gate = _dot(_sigmoid(lr), g2_ref[...])
    if first:
        vout_ref[...] = v
    else:
        vout_ref[...] = jnp.zeros_like(vout_ref)
        mix = _sigmoid(v0_ref[...] + _dot(_dot(v, v1_ref[...]), v2_ref[...]))
        v = v + (vf_ref[...] - v) * mix

    ri, ci = _iota2(W, W)
    seg = jnp.where((ri // HEAD_DIM) == (ci // HEAD_DIM), 1.0, 0.0).astype(BF16)
    kk = k * kk_ref[...]
    kk = kk / jnp.maximum(jnp.sqrt(_dot_sel(kk * kk, seg)), 1e-12)
    k = k * (1.0 + (iclr - 1.0) * ka_ref[...])
    a = -kk
    b = kk * iclr

    same, incl, strict = _chunk_masks(L)
    g = _sel_dot(jnp.where(incl, 1.0, 0.0).astype(BF16), lw)
    gtot = _sel_dot(jnp.where(same, 1.0, 0.0).astype(BF16), lw)
    e_neg = jnp.exp(-g)
    e_rem = jnp.exp(gtot - g)
    r_t = r * jnp.exp(g)
    a_t = a * jnp.exp(g - lw)
    b_t = b * e_neg
    k_t = k * e_neg
    b_e = b * e_rem
    k_e = k * e_rem
    w_l = jnp.exp(gtot)
    eye = jnp.where(_iota2(L, L)[0] == _iota2(L, L)[1], 1.0, 0.0)

    heads = []
    for h in range(nh):
        hs = slice(h * HEAD_DIM, (h + 1) * HEAD_DIM)
        big = _dot_nt(jnp.concatenate([a_t[:, hs], r_t[:, hs]], axis=0),
                      jnp.concatenate([b_t[:, hs], k_t[:, hs]], axis=0))
        a_ab = jnp.where(strict, big[0:L, 0:L], 0.0)
        a_ak = jnp.where(strict, big[0:L, L:], 0.0)
        a_rb = jnp.where(incl, big[L:, 0:L], 0.0)
        a_rk = jnp.where(incl, big[L:, L:], 0.0)
        vh = v[:, hs]
        akv = _dot(a_ak, vh)
        arkv = _dot(a_rk, vh)
        tinv = eye + a_ab
        ak = a_ab
        for _ in range(CHUNK.bit_length() - 2):
            ak = _dot2(ak, ak)
            tinv = tinv + _dot2(tinv, ak)
        st = state_ref[h]
        parts = []
        for c in range(L // CHUNK):
            rs = slice(c * CHUNK, (c + 1) * CHUNK)
            xs_ = _dot_nt(jnp.concatenate([a_t[rs, hs], r_t[rs, hs]], axis=0), st)
            zc = _dot(tinv[rs, rs], xs_[0:CHUNK] + akv[rs])
            parts.append(xs_[CHUNK:] + _dot(a_rb[rs, rs], zc) + arkv[rs])
            st = st * w_l[c * CHUNK:c * CHUNK + 1, hs] + _dot_tn(
                jnp.concatenate([zc, vh[rs]], axis=0),
                jnp.concatenate([b_e[rs, hs], k_e[rs, hs]], axis=0))
        state_ref[h] = st
        heads.append(jnp.concatenate(parts, axis=0))
    y = jnp.concatenate(heads, axis=1)

    inv = 1.0 / HEAD_DIM
    mean = _dot_sel(y, seg) * inv
    yc = y - mean
    var = _dot_sel(yc * yc, seg) * inv
    y = yc * lax.rsqrt(var + RWKV_LNX_EPS) * lg_ref[...] + lb_ref[...]
    y = y + _dot_sel(r * k * rk_ref[...], seg) * v
    y_ref[...] = y * gate


def _rwkv(cols, v_first, prm, layer):
    b, t, _ = cols.shape
    L = ROW_BLOCK
    W = RWKV_WIDTH
    first = v_first is None
    row = lambda bi, ti: (bi, ti, 0)
    par = lambda bi, ti: (layer, 0, 0)
    parv = lambda bi, ti: (max(layer - 1, 0), 0, 0)
    if first:
        v_first = cols
        vf_spec = pl.BlockSpec((None, 8, SMALL), lambda bi, ti: (0, 0, 0))
    else:
        vf_spec = pl.BlockSpec((None, L, W), row)
    vec = lambda n: pl.BlockSpec((None, 1, n), par)
    y, vout = pl.pallas_call(
        functools.partial(_rwkv_kernel, first),
        grid=(b, t // L),
        in_specs=[
            pl.BlockSpec((None, L, RWKV_COLS), row),
            vf_spec,
            vec(RWKV_COLS), vec(W),
            pl.BlockSpec((None, RWKV_LR, W), par),
            vec(W),
            pl.BlockSpec((None, RWKV_LR, W), par),
            pl.BlockSpec((None, RWKV_LR, W), par),
            vec(W), vec(W), vec(W), vec(W), vec(W),
            pl.BlockSpec((None, 1, W), parv),
            pl.BlockSpec((None, W, SMALL), parv),
            pl.BlockSpec((None, SMALL, W), parv),
        ],
        out_specs=[pl.BlockSpec((None, L, W), row),
                   pl.BlockSpec((None, L, W), row) if first
                   else pl.BlockSpec((None, 8, SMALL), lambda bi, ti: (0, 0, 0))],
        out_shape=[jax.ShapeDtypeStruct((b, t, W), F32),
                   jax.ShapeDtypeStruct((b, t, W), F32) if first
                   else jax.ShapeDtypeStruct((1, 8, SMALL), F32)],
        scratch_shapes=[
            pltpu.VMEM((8, RWKV_COLS), F32),
            pltpu.VMEM((W // HEAD_DIM, HEAD_DIM, HEAD_DIM), F32),
        ],
        compiler_params=pltpu.CompilerParams(
            dimension_semantics=("parallel", "arbitrary"), vmem_limit_bytes=VMEM_LIMIT),
        name="rwkv7_mixer",
    )(cols, v_first, *prm)
    return y, vout


def _post_kernel(final, h_ref, ys_ref, yg_ref, yr_ref, p_ref, wo_ref, gm_ref, wu_ref, wd_ref,
                 gp_ref, wg_ref, wp_ref, gf_ref, o_ref):
    d_ff = wu_ref.shape[-1]
    d = h_ref.shape[-1]
    o = 0
    mix = None
    for ref in (ys_ref, yg_ref, yr_ref):
        n = ref.shape[-1]
        t = jnp.dot(ref[...].astype(BF16), wo_ref[o:o + n, :], preferred_element_type=F32)
        mix = t if mix is None else mix + t
        o += n
    h = h_ref[...] + mix
    hn = _rms(h, gm_ref[...]).astype(BF16)
    mlp = None
    for j in range(d_ff // d):
        u = jnp.dot(hn, wu_ref[:, j * d:(j + 1) * d], preferred_element_type=F32)
        u = jnp.square(jnp.maximum(u, 0.0)).astype(BF16)
        t = jnp.dot(u, wd_ref[j * d:(j + 1) * d, :], preferred_element_type=F32)
        mlp = t if mlp is None else mlp + t
    h = h + mlp
    gate = _sigmoid(jnp.dot(_rms(h, gp_ref[...]).astype(BF16), wg_ref[...],
                            preferred_element_type=F32))
    h = h + gate * jnp.dot(p_ref[...].astype(BF16), wp_ref[...], preferred_element_type=F32)
    if final:
        h = _rms(h, gf_ref[...])
    o_ref[...] = h


def _post(h, ys, yg, yr, p, wo, gm, wu, wd, gp, wg, wp, gf, layer, final):
    b, t, d = h.shape
    tm = ROW_BLOCK
    row = lambda bi, ti: (bi, ti, 0)
    par = lambda bi, ti: (layer, 0, 0)

    def wspec(a):
        return pl.BlockSpec((None,) + a.shape[1:], par)

    return pl.pallas_call(
        functools.partial(_post_kernel, final),
        grid=(b, t // tm),
        in_specs=[
            pl.BlockSpec((None, tm, d), row),
            pl.BlockSpec((None, tm, ys.shape[-1]), row),
            pl.BlockSpec((None, tm, yg.shape[-1]), row),
            pl.BlockSpec((None, tm, yr.shape[-1]), row),
            pl.BlockSpec((None, None, tm, p.shape[-1]), lambda bi, ti: (layer, bi, ti, 0)),
            wspec(wo), wspec(gm), wspec(wu), wspec(wd), wspec(gp), wspec(wg), wspec(wp),
            pl.BlockSpec((1, d), lambda bi, ti: (0, 0)),
        ],
        out_specs=pl.BlockSpec((None, tm, d), row),
        out_shape=jax.ShapeDtypeStruct((b, t, d), F32),
        compiler_params=pltpu.CompilerParams(
            dimension_semantics=("parallel", "parallel"), vmem_limit_bytes=VMEM_LIMIT),
        name="post",
    )(h, ys, yg, yr, p, wo, gm, wu, wd, gp, wg, wp, gf)


def _rows(a):
    return a.reshape(a.shape[0], 1, -1)


def _pad_to(a, axis, n, offset=0):
    pads = [(0, 0)] * a.ndim
    pads[axis] = (offset, n - a.shape[axis] - offset)
    return jnp.pad(a, pads)


def kernel(x, p, norm_mix_g, w_in, ssd_conv_w, ssd_conv_b, ssd_dt_bias, ssd_a_log, ssd_d, ssd_norm_g, gla_alpha_w, gla_alpha_b, gla_norm_g, rwkv_mu, rwkv_w0, rwkv_w2, rwkv_a0, rwkv_a2, rwkv_g2, rwkv_k_k, rwkv_k_a, rwkv_r_k, rwkv_lnx_g, rwkv_lnx_b, rwkv_v0, rwkv_v1, rwkv_v2, w_out, norm_mlp_g, w_up, w_down, norm_ple_g, w_ple_gate, w_ple_proj, norm_final_g):
    depth = w_in.shape[0]
    assert x.shape[1] % ROW_BLOCK == 0 and ROW_BLOCK % CHUNK == 0

    ssd_cols = SSD_WIDTH + SSD_XBC + SSD_HEADS
    gla_cols = 4 * GLA_WIDTH + GLA_GATE_RANK
    s0, g0, r0 = 0, ssd_cols, ssd_cols + gla_cols
    small_w = jnp.concatenate(
        [w_in[:, :, s0 + SSD_WIDTH + SSD_XBC:g0], w_in[:, :, g0 + 4 * GLA_WIDTH:r0]], axis=-1)
    w_in_p = jnp.concatenate(
        [w_in[:, :, s0:s0 + SSD_WIDTH + SSD_XBC], w_in[:, :, g0:g0 + 4 * GLA_WIDTH],
         w_in[:, :, r0:], _pad_to(small_w, 2, SMALL)], axis=-1).astype(BF16)

    dtb = _rows(_pad_to(ssd_dt_bias, 1, SMALL))
    aneg = _rows(_pad_to(-jnp.exp(ssd_a_log), 1, SMALL))
    dskip = _rows(jnp.repeat(ssd_d, HEAD_DIM, axis=1))
    gla_aw = _pad_to(gla_alpha_w, 1, SMALL, offset=SSD_HEADS)
    gla_ng = _rows(jnp.tile(gla_norm_g, (1, GLA_WIDTH // HEAD_DIM)))
    w2p = _pad_to(rwkv_w2, 1, RWKV_LR, offset=0)
    a2p = _pad_to(rwkv_a2, 1, RWKV_LR, offset=RWKV_DECAY_RANK)
    g2p = _pad_to(rwkv_g2, 1, RWKV_LR, offset=RWKV_DECAY_RANK + RWKV_ICLR_RANK)
    v1p = _pad_to(rwkv_v1, 2, SMALL)
    v2p = _pad_to(rwkv_v2, 1, SMALL)
    rwkv_prm = (_rows(rwkv_mu), _rows(rwkv_w0), w2p, _rows(rwkv_a0), a2p, g2p,
                _rows(rwkv_k_k), _rows(rwkv_k_a), _rows(rwkv_r_k.reshape(depth, -1)),
                _rows(rwkv_lnx_g), _rows(rwkv_lnx_b), _rows(rwkv_v0), v1p, v2p)

    wo = w_out.astype(BF16)
    wu = w_up.astype(BF16)
    wd = w_down.astype(BF16)
    wg = w_ple_gate.astype(BF16)
    wp = w_ple_proj.astype(BF16)
    gmix, gmlp, gple = _rows(norm_mix_g), _rows(norm_mlp_g), _rows(norm_ple_g)
    gfin = norm_final_g.reshape(1, -1)
    cb = _rows(ssd_conv_b)
    ssd_ng = _rows(ssd_norm_g)
    gla_ab = _rows(gla_alpha_b)

    h = x
    v_first = None
    for i in range(depth):
        c_ssd, c_gla, c_rwkv, c_small = _in_proj(h, gmix, w_in_p, i)
        y_ssd = _ssd(c_ssd, c_small, ssd_conv_w, cb, dtb, aneg, dskip, ssd_ng, i)
        y_gla = _gla(c_gla, c_small, gla_aw, gla_ab, gla_ng, i)
        y_rwkv, vout = _rwkv(c_rwkv, v_first, rwkv_prm, i)
        if i == 0:
            v_first = vout
        h = _post(h, y_ssd, y_gla, y_rwkv, p, wo, gmlp, wu, wd, gple, wg, wp, gfin, i,
                  final=(i == depth - 1))
    return h
```

```python
import functools

import jax
import jax.numpy as jnp
from jax import lax
from jax.experimental import pallas as pl
from jax.experimental.pallas import tpu as pltpu

F32 = jnp.float32
BF16 = jnp.bfloat16

NORM_EPS = 1e-6
HEAD_DIM = 64
SSD_HEADS = 8
SSD_GROUPS = 2
SSD_STATE = 128
SSD_WIDTH = SSD_HEADS * HEAD_DIM
SSD_XBC = SSD_WIDTH + 2 * SSD_GROUPS * SSD_STATE
SSD_CONV = 4
GLA_WIDTH = 4 * HEAD_DIM
GLA_GATE_RANK = 16
GLA_GATE_NORMALIZER = 16.0
RWKV_WIDTH = 4 * HEAD_DIM
RWKV_DECAY_RANK = 32
RWKV_ICLR_RANK = 32
RWKV_GATE_RANK = 64
RWKV_LR = RWKV_DECAY_RANK + RWKV_ICLR_RANK + RWKV_GATE_RANK
RWKV_COLS = 3 * RWKV_WIDTH + RWKV_LR
RWKV_LNX_EPS = 64e-5
SMALL = 128

CHUNK = 64
ROW_BLOCK = 256
VMEM_LIMIT = 56 * 1024 * 1024


def _dot(a, b):
    return jnp.dot(a.astype(BF16), b.astype(BF16), preferred_element_type=F32)


def _dot_nt(a, b):
    return lax.dot_general(a.astype(BF16), b.astype(BF16), (((1,), (1,)), ((), ())),
                           preferred_element_type=F32)


def _split(a, pieces):
    out = []
    r = a
    for _ in range(pieces):
        p = r.astype(BF16)
        out.append(p)
        r = r - p.astype(F32)
    return out


def _sel_dot(m01, a, pieces=2):
    acc = None
    for p in _split(a, pieces):
        t = jnp.dot(m01, p, preferred_element_type=F32)
        acc = t if acc is None else acc + t
    return acc


def _dot_sel(a, m01, pieces=1):
    acc = None
    for p in _split(a, pieces):
        t = jnp.dot(p, m01, preferred_element_type=F32)
        acc = t if acc is None else acc + t
    return acc


def _iota2(n, m):
    return (lax.broadcasted_iota(jnp.int32, (n, m), 0),
            lax.broadcasted_iota(jnp.int32, (n, m), 1))


def _softplus(x):
    return jnp.maximum(x, 0.0) + jnp.log1p(jnp.exp(-jnp.abs(x)))


def _sigmoid(x):
    return 1.0 / (1.0 + jnp.exp(-x))


def _rms(x, g):
    return x * lax.rsqrt(jnp.mean(x * x, axis=-1, keepdims=True) + NORM_EPS) * g


def _in_proj_kernel(h_ref, g_ref, w_ref, ssd_ref, gla_ref, rwkv_ref, small_ref):
    xn = _rms(h_ref[...], g_ref[...]).astype(BF16)
    o = 0
    for ref in (ssd_ref, gla_ref, rwkv_ref, small_ref):
        n = ref.shape[-1]
        ref[...] = jnp.dot(xn, w_ref[:, o:o + n], preferred_element_type=F32)
        o += n


def _in_proj(h, g, w, layer):
    b, t, d = h.shape
    tm = ROW_BLOCK
    widths = (SSD_WIDTH + SSD_XBC, 4 * GLA_WIDTH, RWKV_COLS, SMALL)
    total = sum(widths)
    row = lambda bi, ti: (bi, ti, 0)
    return pl.pallas_call(
        _in_proj_kernel,
        grid=(b, t // tm),
        in_specs=[
            pl.BlockSpec((None, tm, d), row),
            pl.BlockSpec((None, 1, d), lambda bi, ti: (layer, 0, 0)),
            pl.BlockSpec((None, d, total), lambda bi, ti: (layer, 0, 0)),
        ],
        out_specs=[pl.BlockSpec((None, tm, n), row) for n in widths],
        out_shape=[jax.ShapeDtypeStruct((b, t, n), F32) for n in widths],
        compiler_params=pltpu.CompilerParams(
            dimension_semantics=("parallel", "parallel"), vmem_limit_bytes=VMEM_LIMIT),
        name="in_proj",
    )(h, g, w)


def _ssd_kernel(main_ref, small_ref, cw_ref, cb_ref, dtb_ref, aneg_ref, dskip_ref, ng_ref,
                y_ref, tail_ref, state_ref):
    L = main_ref.shape[0]
    hg = SSD_HEADS // SSD_GROUPS

    @pl.when(pl.program_id(1) == 0)
    def _():
        tail_ref[...] = jnp.zeros_like(tail_ref)
        state_ref[...] = jnp.zeros_like(state_ref)

    z = main_ref[:, 0:SSD_WIDTH]
    u = main_ref[:, SSD_WIDTH:SSD_WIDTH + SSD_XBC]
    full = jnp.concatenate([tail_ref[...], u], axis=0)
    acc = u * cw_ref[SSD_CONV - 1:SSD_CONV, :] + cb_ref[...]
    for j in range(1, SSD_CONV):
        acc = acc + pltpu.roll(full, j, axis=0)[8:, :] * cw_ref[SSD_CONV - 1 - j:SSD_CONV - j, :]
    tail_ref[...] = u[L - 8:, :]
    xbc = acc * _sigmoid(acc)
    xs = xbc[:, 0:SSD_WIDTH]
    bm = xbc[:, SSD_WIDTH:SSD_WIDTH + SSD_GROUPS * SSD_STATE]
    cm = xbc[:, SSD_WIDTH + SSD_GROUPS * SSD_STATE:]

    dt = _softplus(small_ref[...] + dtb_ref[...])
    da = dt * aneg_ref[...]
    ri, ci = _iota2(L, L)
    causal = ri >= ci
    tril = jnp.where(causal, 1.0, 0.0).astype(BF16)
    cs = _sel_dot(tril, da, pieces=3)
    cs_t = cs.T
    cs_last = cs[L - 1:L, :]
    e_cs = jnp.exp(cs)
    e_rem = jnp.exp(cs_last - cs)
    e_last = jnp.exp(cs_last)

    ys = []
    for g in range(SSD_GROUPS):
        bg_t = bm[:, g * SSD_STATE:(g + 1) * SSD_STATE].T
        cg = cm[:, g * SSD_STATE:(g + 1) * SSD_STATE]
        cb = _dot(cg, bg_t)
        st = state_ref[g]
        y_off = _dot(cg, st)
        xws, decs = [], []
        for hh in range(hg):
            h = g * hg + hh
            seg = cs[:, h:h + 1] - cs_t[h:h + 1, :]
            lmat = jnp.exp(jnp.where(causal, seg, -jnp.inf))
            xdt = xs[:, h * HEAD_DIM:(h + 1) * HEAD_DIM] * dt[:, h:h + 1]
            y_h = _dot(cb * lmat, xdt) + y_off[:, hh * HEAD_DIM:(hh + 1) * HEAD_DIM] * e_cs[:, h:h + 1]
            ys.append(y_h)
            xws.append(xdt * e_rem[:, h:h + 1])
            decs.append(jnp.broadcast_to(e_last[:, h:h + 1], (1, HEAD_DIM)))
        inc = _dot(bg_t, jnp.concatenate(xws, axis=1))
        state_ref[g] = st * jnp.concatenate(decs, axis=1) + inc

    y = jnp.concatenate(ys, axis=1) + xs * dskip_ref[...]
    y = y * (z * _sigmoid(z))
    gw = SSD_WIDTH // SSD_GROUPS
    for g in range(SSD_GROUPS):
        sl = slice(g * gw, (g + 1) * gw)
        y_ref[:, sl] = _rms(y[:, sl], ng_ref[:, sl])


def _ssd(main, small, cw, cb, dtb, aneg, dskip, ng, layer):
    b, t, _ = main.shape
    L = ROW_BLOCK
    row = lambda bi, ti: (bi, ti, 0)
    par = lambda bi, ti: (layer, 0, 0)
    return pl.pallas_call(
        _ssd_kernel,
        grid=(b, t // L),
        in_specs=[
            pl.BlockSpec((None, L, main.shape[-1]), row),
            pl.BlockSpec((None, L, SMALL), row),
            pl.BlockSpec((None, SSD_CONV, SSD_XBC), par),
            pl.BlockSpec((None, 1, SSD_XBC), par),
            pl.BlockSpec((None, 1, SMALL), par),
            pl.BlockSpec((None, 1, SMALL), par),
            pl.BlockSpec((None, 1, SSD_WIDTH), par),
            pl.BlockSpec((None, 1, SSD_WIDTH), par),
        ],
        out_specs=pl.BlockSpec((None, L, SSD_WIDTH), row),
        out_shape=jax.ShapeDtypeStruct((b, t, SSD_WIDTH), F32),
        scratch_shapes=[
            pltpu.VMEM((8, SSD_XBC), F32),
            pltpu.VMEM((SSD_GROUPS, SSD_STATE, SSD_WIDTH // SSD_GROUPS), F32),
        ],
        compiler_params=pltpu.CompilerParams(
            dimension_semantics=("parallel", "arbitrary"), vmem_limit_bytes=VMEM_LIMIT),
        name="ssd_mixer",
    )(main, small, cw, cb, dtb, aneg, dskip, ng)


def _chunk_masks(L):
    ri, ci = _iota2(L, L)
    same = (ri // CHUNK) == (ci // CHUNK)
    return same, same & (ri >= ci), same & (ri > ci)


def _gla_kernel(main_ref, small_ref, aw_ref, ab_ref, ng_ref, y_ref, state_ref):
    L = main_ref.shape[0]
    nh = GLA_WIDTH // HEAD_DIM
    W = GLA_WIDTH

    @pl.when(pl.program_id(1) == 0)
    def _():
        state_ref[...] = jnp.zeros_like(state_ref)

    q = main_ref[:, 0:W] * (HEAD_DIM ** -0.5)
    k = main_ref[:, W:2 * W]
    v = main_ref[:, 2 * W:3 * W]
    gg = main_ref[:, 3 * W:4 * W]
    x = _dot(small_ref[...], aw_ref[...]) + ab_ref[...]
    la = -_softplus(-x) * (1.0 / GLA_GATE_NORMALIZER)
    _, incl, _ = _chunk_masks(L)
    nc = L // CHUNK
    bcum = _sel_dot(jnp.where(incl, 1.0, 0.0).astype(BF16), la)
    btot = jnp.concatenate(
        [jnp.broadcast_to(bcum[(c + 1) * CHUNK - 1:(c + 1) * CHUNK, :], (CHUNK, W)) for c in range(nc)],
        axis=0)
    q_in = q * jnp.exp(bcum)
    k_in = k * jnp.exp(-bcum)
    k_end = k * jnp.exp(btot - bcum)
    dec = jnp.exp(btot)
    ri, ci = _iota2(W, W)
    same_head = (ri // HEAD_DIM) == (ci // HEAD_DIM)
    seg = jnp.where(same_head, 1.0, 0.0).astype(BF16)

    heads = []
    for h in range(nh):
        hs = slice(h * HEAD_DIM, (h + 1) * HEAD_DIM)
        scores = jnp.where(incl, _dot_nt(q_in[:, hs], k_in[:, hs]), 0.0)
        heads.append(_dot(scores, v[:, hs]))
    st = state_ref[...]
    parts = []
    for c in range(nc):
        rs = slice(c * CHUNK, (c + 1) * CHUNK)
        parts.append(_dot(q_in[rs, :], st))
        kv = jnp.where(same_head, _dot(k_end[rs, :].T, v[rs, :]), 0.0)
        st = st * dec[c * CHUNK:c * CHUNK + 8, :].T[:, 0:1] + kv
    state_ref[...] = st
    o = jnp.concatenate(heads, axis=1) + jnp.concatenate(parts, axis=0)

    ms = _dot_sel(o * o, seg) * (1.0 / HEAD_DIM)
    o = o * lax.rsqrt(ms + NORM_EPS) * ng_ref[...]
    y_ref[...] = o * (gg * _sigmoid(gg))


def _gla(main, small, aw, ab, ng, layer):
    b, t, _ = main.shape
    L = ROW_BLOCK
    row = lambda bi, ti: (bi, ti, 0)
    par = lambda bi, ti: (layer, 0, 0)
    return pl.pallas_call(
        _gla_kernel,
        grid=(b, t // L),
        in_specs=[
            pl.BlockSpec((None, L, 4 * GLA_WIDTH), row),
            pl.BlockSpec((None, L, SMALL), row),
            pl.BlockSpec((None, SMALL, GLA_WIDTH), par),
            pl.BlockSpec((None, 1, GLA_WIDTH), par),
            pl.BlockSpec((None, 1, GLA_WIDTH), par),
        ],
        out_specs=pl.BlockSpec((None, L, GLA_WIDTH), row),
        out_shape=jax.ShapeDtypeStruct((b, t, GLA_WIDTH), F32),
        scratch_shapes=[pltpu.VMEM((GLA_WIDTH, GLA_WIDTH), F32)],
        compiler_params=pltpu.CompilerParams(
            dimension_semantics=("parallel", "arbitrary"), vmem_limit_bytes=VMEM_LIMIT),
        name="gla_mixer",
    )(main, small, aw, ab, ng)


def _rwkv_kernel(first, cols_ref, vf_ref, mu_ref, w0_ref, w2_ref, a0_ref, a2_ref, g2_ref,
                 kk_ref, ka_ref, rk_ref, lg_ref, lb_ref, v0_ref, v1_ref, v2_ref,
                 y_ref, vout_ref, prev_ref, state_ref):
    L = cols_ref.shape[0]
    W = RWKV_WIDTH
    D = HEAD_DIM
    nh = W // D
    nc = L // CHUNK

    @pl.when(pl.program_id(1) == 0)
    def _():
        prev_ref[...] = jnp.zeros_like(prev_ref)
        state_ref[...] = jnp.zeros_like(state_ref)

    cols = cols_ref[...]
    full = jnp.concatenate([prev_ref[...], cols], axis=0)
    shifted = pltpu.roll(full, 1, axis=0)[8:, :]
    prev_ref[...] = cols[L - 8:, :]
    xx = cols + (shifted - cols) * mu_ref[...]
    r = xx[:, 0:W]
    k = xx[:, W:2 * W]
    v = xx[:, 2 * W:3 * W]
    lr = xx[:, 3 * W:]
    log_w = -_softplus(-(w0_ref[...] + _dot(jnp.tanh(lr), w2_ref[...]))) - 0.5
    lw = -jnp.exp(log_w)
    iclr = _sigmoid(a0_ref[...] + _dot(lr, a2_ref[...]))
    gate = _dot(_sigmoid(lr), g2_ref[...])
    if first:
        vout_ref[...] = v
    else:
        vout_ref[...] = jnp.zeros_like(vout_ref)
        mix = _sigmoid(v0_ref[...] + _dot(_dot(v, v1_ref[...]), v2_ref[...]))
        v = v + (vf_ref[...] - v) * mix

    ri, ci = _iota2(W, W)
    same_head = (ri // D) == (ci // D)
    seg = jnp.where(same_head, 1.0, 0.0).astype(BF16)
    kk = k * kk_ref[...]
    kk = kk / jnp.maximum(jnp.sqrt(_dot_sel(kk * kk, seg)), 1e-12)
    k = k * (1.0 + (iclr - 1.0) * ka_ref[...])
    b = kk * iclr

    _, incl, strict = _chunk_masks(L)
    g = _sel_dot(jnp.where(incl, 1.0, 0.0).astype(BF16), lw)
    gtot = jnp.concatenate(
        [jnp.broadcast_to(g[(c + 1) * CHUNK - 1:(c + 1) * CHUNK, :], (CHUNK, W)) for c in range(nc)],
        axis=0)
    e_neg = jnp.exp(-g)
    e_rem = jnp.exp(gtot - g)
    r_t = r * jnp.exp(g)
    a_t = -kk * jnp.exp(g - lw)
    b_t = b * e_neg
    k_t = k * e_neg
    b_e = b * e_rem
    k_e = k * e_rem
    w_l = jnp.exp(gtot)
    eye = jnp.where(_iota2(L, L)[0] == _iota2(L, L)[1], 1.0, 0.0)

    hsl = [slice(h * D, (h + 1) * D) for h in range(nh)]
    a_ab, a_rb, akv, arkv = [], [], [], []
    for hs in hsl:
        big = _dot_nt(jnp.concatenate([a_t[:, hs], r_t[:, hs]], axis=0),
                      jnp.concatenate([b_t[:, hs], k_t[:, hs]], axis=0))
        a_ab.append(jnp.where(strict, big[0:L, 0:L], 0.0))
        a_rb.append(jnp.where(incl, big[L:, 0:L], 0.0))
        akv.append(_dot(jnp.where(strict, big[0:L, L:], 0.0), v[:, hs]))
        arkv.append(_dot(jnp.where(incl, big[L:, L:], 0.0), v[:, hs]))
    tinv = [eye + a for a in a_ab]
    ak = a_ab
    for _ in range(CHUNK.bit_length() - 2):
        ak = [_dot(a, a) for a in ak]
        tinv = [t + _dot(t, a) for t, a in zip(tinv, ak)]
    qz = [_dot(tinv[h], jnp.concatenate([a_t[:, hs], akv[h]], axis=1)) for h, hs in enumerate(hsl)]
    ry = [_dot(a_rb[h], qz[h]) for h in range(nh)]
    qa = jnp.concatenate([q[:, 0:D] for q in qz], axis=1)
    zf = jnp.concatenate([q[:, D:] for q in qz], axis=1)
    rq = r_t + jnp.concatenate([x[:, 0:D] for x in ry], axis=1)
    yf = jnp.concatenate([x[:, D:] + arkv[h] for h, x in enumerate(ry)], axis=1)

    p_c, n_c, w_c = [], [], []
    for c in range(nc):
        rs = slice(c * CHUNK, (c + 1) * CHUNK)
        b_tr = b_e[rs, :].T
        k_tr = k_e[rs, :].T
        p_c.append(jnp.where(same_head, _dot(b_tr, qa[rs, :]), 0.0))
        n_c.append(jnp.where(same_head, _dot(jnp.concatenate([b_tr, k_tr], axis=1),
                                             jnp.concatenate([zf[rs, :], v[rs, :]], axis=0)), 0.0))
        w_c.append(w_l[c * CHUNK:c * CHUNK + 8, :].T[:, 0:1])
    st = state_ref[...]
    ys = []
    for c in range(nc):
        rs = slice(c * CHUNK, (c + 1) * CHUNK)
        ys.append(_dot(rq[rs, :], st) + yf[rs, :])
        st = st * w_c[c] + _dot(p_c[c], st) + n_c[c]
    state_ref[...] = st
    y = jnp.concatenate(ys, axis=0)

    inv = 1.0 / D
    mean = _dot_sel(y, seg) * inv
    yc = y - mean
    var = _dot_sel(yc * yc, seg) * inv
    y = yc * lax.rsqrt(var + RWKV_LNX_EPS) * lg_ref[...] + lb_ref[...]
    y = y + _dot_sel(r * k * rk_ref[...], seg) * v
    y_ref[...] = y * gate


def _rwkv(cols, v_first, prm, layer):
    b, t, _ = cols.shape
    L = ROW_BLOCK
    W = RWKV_WIDTH
    first = v_first is None
    row = lambda bi, ti: (bi, ti, 0)
    par = lambda bi, ti: (layer, 0, 0)
    parv = lambda bi, ti: (max(layer - 1, 0), 0, 0)
    dummy = pl.BlockSpec((None, 8, SMALL), lambda bi, ti: (0, 0, 0))
    if first:
        v_first = cols
    vec = lambda n: pl.BlockSpec((None, 1, n), par)
    y, vout = pl.pallas_call(
        functools.partial(_rwkv_kernel, first),
        grid=(b, t // L),
        in_specs=[
            pl.BlockSpec((None, L, RWKV_COLS), row),
            dummy if first else pl.BlockSpec((None, L, W), row),
            vec(RWKV_COLS), vec(W),
            pl.BlockSpec((None, RWKV_LR, W), par),
            vec(W),
            pl.BlockSpec((None, RWKV_LR, W), par),
            pl.BlockSpec((None, RWKV_LR, W), par),
            vec(W), vec(W), vec(W), vec(W), vec(W),
            pl.BlockSpec((None, 1, W), parv),
            pl.BlockSpec((None, W, SMALL), parv),
            pl.BlockSpec((None, SMALL, W), parv),
        ],
        out_specs=[pl.BlockSpec((None, L, W), row),
                   pl.BlockSpec((None, L, W), row) if first else dummy],
        out_shape=[jax.ShapeDtypeStruct((b, t, W), F32),
                   jax.ShapeDtypeStruct((b, t, W), F32) if first
                   else jax.ShapeDtypeStruct((1, 8, SMALL), F32)],
        scratch_shapes=[
            pltpu.VMEM((8, RWKV_COLS), F32),
            pltpu.VMEM((W, W), F32),
        ],
        compiler_params=pltpu.CompilerParams(
            dimension_semantics=("parallel", "arbitrary"), vmem_limit_bytes=VMEM_LIMIT),
        name="rwkv7_mixer",
    )(cols, v_first, *prm)
    return y, vout


def _post_kernel(final, h_ref, ys_ref, yg_ref, yr_ref, p_ref, wo_ref, gm_ref, wu_ref, wd_ref,
                 gp_ref, wg_ref, wp_ref, gf_ref, o_ref):
    d_ff = wu_ref.shape[-1]
    d = h_ref.shape[-1]
    o = 0
    mix = None
    for ref in (ys_ref, yg_ref, yr_ref):
        n = ref.shape[-1]
        t = jnp.dot(ref[...].astype(BF16), wo_ref[o:o + n, :], preferred_element_type=F32)
        mix = t if mix is None else mix + t
        o += n
    h = h_ref[...] + mix
    hn = _rms(h, gm_ref[...]).astype(BF16)
    mlp = None
    for j in range(d_ff // d):
        u = jnp.dot(hn, wu_ref[:, j * d:(j + 1) * d], preferred_element_type=F32)
        u = jnp.square(jnp.maximum(u, 0.0)).astype(BF16)
        t = jnp.dot(u, wd_ref[j * d:(j + 1) * d, :], preferred_element_type=F32)
        mlp = t if mlp is None else mlp + t
    h = h + mlp
    gate = _sigmoid(jnp.dot(_rms(h, gp_ref[...]).astype(BF16), wg_ref[...],
                            preferred_element_type=F32))
    h = h + gate * jnp.dot(p_ref[...].astype(BF16), wp_ref[...], preferred_element_type=F32)
    if final:
        h = _rms(h, gf_ref[...])
    o_ref[...] = h


def _post(h, ys, yg, yr, p, wo, gm, wu, wd, gp, wg, wp, gf, layer, final):
    b, t, d = h.shape
    tm = ROW_BLOCK
    row = lambda bi, ti: (bi, ti, 0)
    par = lambda bi, ti: (layer, 0, 0)

    def wspec(a):
        return pl.BlockSpec((None,) + a.shape[1:], par)

    return pl.pallas_call(
        functools.partial(_post_kernel, final),
        grid=(b, t // tm),
        in_specs=[
            pl.BlockSpec((None, tm, d), row),
            pl.BlockSpec((None, tm, ys.shape[-1]), row),
            pl.BlockSpec((None, tm, yg.shape[-1]), row),
            pl.BlockSpec((None, tm, yr.shape[-1]), row),
            pl.BlockSpec((None, None, tm, p.shape[-1]), lambda bi, ti: (layer, bi, ti, 0)),
            wspec(wo), wspec(gm), wspec(wu), wspec(wd), wspec(gp), wspec(wg), wspec(wp),
            pl.BlockSpec((1, d), lambda bi, ti: (0, 0)),
        ],
        out_specs=pl.BlockSpec((None, tm, d), row),
        out_shape=jax.ShapeDtypeStruct((b, t, d), F32),
        compiler_params=pltpu.CompilerParams(
            dimension_semantics=("parallel", "parallel"), vmem_limit_bytes=VMEM_LIMIT),
        name="post",
    )(h, ys, yg, yr, p, wo, gm, wu, wd, gp, wg, wp, gf)


def _rows(a):
    return a.reshape(a.shape[0], 1, -1)


def _pad_to(a, axis, n, offset=0):
    pads = [(0, 0)] * a.ndim
    pads[axis] = (offset, n - a.shape[axis] - offset)
    return jnp.pad(a, pads)


def kernel(x, p, norm_mix_g, w_in, ssd_conv_w, ssd_conv_b, ssd_dt_bias, ssd_a_log, ssd_d, ssd_norm_g, gla_alpha_w, gla_alpha_b, gla_norm_g, rwkv_mu, rwkv_w0, rwkv_w2, rwkv_a0, rwkv_a2, rwkv_g2, rwkv_k_k, rwkv_k_a, rwkv_r_k, rwkv_lnx_g, rwkv_lnx_b, rwkv_v0, rwkv_v1, rwkv_v2, w_out, norm_mlp_g, w_up, w_down, norm_ple_g, w_ple_gate, w_ple_proj, norm_final_g):
    depth = w_in.shape[0]
    assert x.shape[1] % ROW_BLOCK == 0 and ROW_BLOCK % CHUNK == 0

    ssd_cols = SSD_WIDTH + SSD_XBC + SSD_HEADS
    gla_cols = 4 * GLA_WIDTH + GLA_GATE_RANK
    s0, g0, r0 = 0, ssd_cols, ssd_cols + gla_cols
    small_w = jnp.concatenate(
        [w_in[:, :, s0 + SSD_WIDTH + SSD_XBC:g0], w_in[:, :, g0 + 4 * GLA_WIDTH:r0]], axis=-1)
    w_in_p = jnp.concatenate(
        [w_in[:, :, s0:s0 + SSD_WIDTH + SSD_XBC], w_in[:, :, g0:g0 + 4 * GLA_WIDTH],
         w_in[:, :, r0:], _pad_to(small_w, 2, SMALL)], axis=-1).astype(BF16)

    dtb = _rows(_pad_to(ssd_dt_bias, 1, SMALL))
    aneg = _rows(_pad_to(-jnp.exp(ssd_a_log), 1, SMALL))
    dskip = _rows(jnp.repeat(ssd_d, HEAD_DIM, axis=1))
    gla_aw = _pad_to(gla_alpha_w, 1, SMALL, offset=SSD_HEADS)
    gla_ng = _rows(jnp.tile(gla_norm_g, (1, GLA_WIDTH // HEAD_DIM)))
    w2p = _pad_to(rwkv_w2, 1, RWKV_LR, offset=0)
    a2p = _pad_to(rwkv_a2, 1, RWKV_LR, offset=RWKV_DECAY_RANK)
    g2p = _pad_to(rwkv_g2, 1, RWKV_LR, offset=RWKV_DECAY_RANK + RWKV_ICLR_RANK)
    v1p = _pad_to(rwkv_v1, 2, SMALL)
    v2p = _pad_to(rwkv_v2, 1, SMALL)
    rwkv_prm = (_rows(rwkv_mu), _rows(rwkv_w0), w2p, _rows(rwkv_a0), a2p, g2p,
                _rows(rwkv_k_k), _rows(rwkv_k_a), _rows(rwkv_r_k.reshape(depth, -1)),
                _rows(rwkv_lnx_g), _rows(rwkv_lnx_b), _rows(rwkv_v0), v1p, v2p)

    wo = w_out.astype(BF16)
    wu = w_up.astype(BF16)
    wd = w_down.astype(BF16)
    wg = w_ple_gate.astype(BF16)
    wp = w_ple_proj.astype(BF16)
    gmix, gmlp, gple = _rows(norm_mix_g), _rows(norm_mlp_g), _rows(norm_ple_g)
    gfin = norm_final_g.reshape(1, -1)
    cb = _rows(ssd_conv_b)
    ssd_ng = _rows(ssd_norm_g)
    gla_ab = _rows(gla_alpha_b)

    h = x
    v_first = None
    for i in range(depth):
        c_ssd, c_gla, c_rwkv, c_small = _in_proj(h, gmix, w_in_p, i)
        y_ssd = _ssd(c_ssd, c_small, ssd_conv_w, cb, dtb, aneg, dskip, ssd_ng, i)
        y_gla = _gla(c_gla, c_small, gla_aw, gla_ab, gla_ng, i)
        y_rwkv, vout = _rwkv(c_rwkv, v_first, rwkv_prm, i)
        if i == 0:
            v_first = vout
        h = _post(h, y_ssd, y_gla, y_rwkv, p, wo, gmlp, wu, wd, gple, wg, wp, gfin, i,
                  final=(i == depth - 1))
    return h
```

```python
import functools
import itertools

import jax
import jax.numpy as jnp
from jax import lax
from jax.experimental import pallas as pl
from jax.experimental.pallas import tpu as pltpu

F32 = jnp.float32
BF16 = jnp.bfloat16

NORM_EPS = 1e-6
HEAD_DIM = 64
SSD_HEADS = 8
SSD_GROUPS = 2
SSD_STATE = 128
SSD_WIDTH = SSD_HEADS * HEAD_DIM
SSD_XBC = SSD_WIDTH + 2 * SSD_GROUPS * SSD_STATE
SSD_CONV = 4
GLA_WIDTH = 4 * HEAD_DIM
GLA_GATE_RANK = 16
GLA_GATE_NORMALIZER = 16.0
RWKV_WIDTH = 4 * HEAD_DIM
RWKV_DECAY_RANK = 32
RWKV_ICLR_RANK = 32
RWKV_GATE_RANK = 64
RWKV_LR = RWKV_DECAY_RANK + RWKV_ICLR_RANK + RWKV_GATE_RANK
RWKV_COLS = 3 * RWKV_WIDTH + RWKV_LR
RWKV_LNX_EPS = 64e-5
SMALL = 128

CHUNK = 64
ROW_BLOCK = 256
VMEM_LIMIT = 56 * 1024 * 1024


def _dot(a, b):
    return jnp.dot(a.astype(BF16), b.astype(BF16), preferred_element_type=F32)


def _dot_nt(a, b):
    return lax.dot_general(a.astype(BF16), b.astype(BF16), (((1,), (1,)), ((), ())),
                           preferred_element_type=F32)


def _split(a, pieces):
    out = []
    r = a
    for _ in range(pieces):
        p = r.astype(BF16)
        out.append(p)
        r = r - p.astype(F32)
    return out


def _sel_dot(m01, a, pieces=2):
    acc = None
    for p in _split(a, pieces):
        t = jnp.dot(m01, p, preferred_element_type=F32)
        acc = t if acc is None else acc + t
    return acc


def _dot_sel(a, m01, pieces=1):
    acc = None
    for p in _split(a, pieces):
        t = jnp.dot(p, m01, preferred_element_type=F32)
        acc = t if acc is None else acc + t
    return acc


def _iota2(n, m):
    return (lax.broadcasted_iota(jnp.int32, (n, m), 0),
            lax.broadcasted_iota(jnp.int32, (n, m), 1))


def _softplus(x):
    return jnp.maximum(x, 0.0) + jnp.log1p(jnp.exp(-jnp.abs(x)))


def _sigmoid(x):
    return 1.0 / (1.0 + jnp.exp(-x))


def _rms(x, g):
    return x * lax.rsqrt(jnp.mean(x * x, axis=-1, keepdims=True) + NORM_EPS) * g


def _in_proj_kernel(h_ref, g_ref, w_ref, ssd_ref, gla_ref, rwkv_ref, small_ref):
    xn = _rms(h_ref[...], g_ref[...]).astype(BF16)
    o = 0
    for ref in (ssd_ref, gla_ref, rwkv_ref, small_ref):
        n = ref.shape[-1]
        ref[...] = jnp.dot(xn, w_ref[:, o:o + n], preferred_element_type=F32)
        o += n


def _in_proj(h, boff, nb, g, w, layer):
    _, t, d = h.shape
    b = nb
    tm = ROW_BLOCK
    widths = (SSD_WIDTH + SSD_XBC, 4 * GLA_WIDTH, RWKV_COLS, SMALL)
    total = sum(widths)
    row = lambda bi, ti: (bi, ti, 0)
    return pl.pallas_call(
        _in_proj_kernel,
        grid=(b, t // tm),
        in_specs=[
            pl.BlockSpec((None, tm, d), lambda bi, ti: (bi + boff, ti, 0)),
            pl.BlockSpec((None, 1, d), lambda bi, ti: (layer, 0, 0)),
            pl.BlockSpec((None, d, total), lambda bi, ti: (layer, 0, 0)),
        ],
        out_specs=[pl.BlockSpec((None, tm, n), row) for n in widths],
        out_shape=[jax.ShapeDtypeStruct((b, t, n), F32) for n in widths],
        compiler_params=pltpu.CompilerParams(
            dimension_semantics=("parallel", "parallel"), vmem_limit_bytes=VMEM_LIMIT),
        name="in_proj",
    )(h, g, w)


def _ssd_body(main_ref, small_ref, cw_ref, cb_ref, dtb_ref, aneg_ref, dskip_ref, ng_ref,
              y_ref, tail_ref, state_ref):
    L = main_ref.shape[0]
    hg = SSD_HEADS // SSD_GROUPS

    z = main_ref[:, 0:SSD_WIDTH]
    u = main_ref[:, SSD_WIDTH:SSD_WIDTH + SSD_XBC]
    full = jnp.concatenate([tail_ref[...], u], axis=0)
    acc = u * cw_ref[SSD_CONV - 1:SSD_CONV, :] + cb_ref[...]
    for j in range(1, SSD_CONV):
        acc = acc + pltpu.roll(full, j, axis=0)[8:, :] * cw_ref[SSD_CONV - 1 - j:SSD_CONV - j, :]
    tail_ref[...] = u[L - 8:, :]
    xbc = acc * _sigmoid(acc)
    xs = xbc[:, 0:SSD_WIDTH]
    bm = xbc[:, SSD_WIDTH:SSD_WIDTH + SSD_GROUPS * SSD_STATE]
    cm = xbc[:, SSD_WIDTH + SSD_GROUPS * SSD_STATE:]

    dt = _softplus(small_ref[...] + dtb_ref[...])
    da = dt * aneg_ref[...]
    ri, ci = _iota2(L, L)
    causal = ri >= ci
    tril = jnp.where(causal, 1.0, 0.0).astype(BF16)
    cs = _sel_dot(tril, da, pieces=3)
    cs_t = cs.T
    cs_last = cs[L - 1:L, :]
    e_cs = jnp.exp(cs)
    e_rem = jnp.exp(cs_last - cs)
    e_last = jnp.exp(cs_last)
    yield

    ys = []
    for g in range(SSD_GROUPS):
        bg_t = bm[:, g * SSD_STATE:(g + 1) * SSD_STATE].T
        cg = cm[:, g * SSD_STATE:(g + 1) * SSD_STATE]
        cb = _dot(cg, bg_t)
        st = state_ref[g]
        y_off = _dot(cg, st)
        xws, decs = [], []
        for hh in range(hg):
            h = g * hg + hh
            seg = cs[:, h:h + 1] - cs_t[h:h + 1, :]
            lmat = jnp.exp(jnp.where(causal, seg, -jnp.inf))
            xdt = xs[:, h * HEAD_DIM:(h + 1) * HEAD_DIM] * dt[:, h:h + 1]
            y_h = _dot(cb * lmat, xdt) + y_off[:, hh * HEAD_DIM:(hh + 1) * HEAD_DIM] * e_cs[:, h:h + 1]
            ys.append(y_h)
            xws.append(xdt * e_rem[:, h:h + 1])
            decs.append(jnp.broadcast_to(e_last[:, h:h + 1], (1, HEAD_DIM)))
        inc = _dot(bg_t, jnp.concatenate(xws, axis=1))
        state_ref[g] = st * jnp.concatenate(decs, axis=1) + inc
        yield

    y = jnp.concatenate(ys, axis=1) + xs * dskip_ref[...]
    y = y * (z * _sigmoid(z))
    gw = SSD_WIDTH // SSD_GROUPS
    for g in range(SSD_GROUPS):
        sl = slice(g * gw, (g + 1) * gw)
        y_ref[:, sl] = _rms(y[:, sl], ng_ref[:, sl])


def _chunk_masks(L):
    ri, ci = _iota2(L, L)
    same = (ri // CHUNK) == (ci // CHUNK)
    return same, same & (ri >= ci), same & (ri > ci)


def _gla_body(main_ref, small_ref, aw_ref, ab_ref, ng_ref, y_ref, state_ref):
    L = main_ref.shape[0]
    nh = GLA_WIDTH // HEAD_DIM
    W = GLA_WIDTH

    q = main_ref[:, 0:W] * (HEAD_DIM ** -0.5)
    k = main_ref[:, W:2 * W]
    v = main_ref[:, 2 * W:3 * W]
    gg = main_ref[:, 3 * W:4 * W]
    x = _dot(small_ref[...], aw_ref[...]) + ab_ref[...]
    la = -_softplus(-x) * (1.0 / GLA_GATE_NORMALIZER)
    _, incl, _ = _chunk_masks(L)
    nc = L // CHUNK
    bcum = _sel_dot(jnp.where(incl, 1.0, 0.0).astype(BF16), la)
    btot = jnp.concatenate(
        [jnp.broadcast_to(bcum[(c + 1) * CHUNK - 1:(c + 1) * CHUNK, :], (CHUNK, W)) for c in range(nc)],
        axis=0)
    q_in = q * jnp.exp(bcum)
    k_in = k * jnp.exp(-bcum)
    k_end = k * jnp.exp(btot - bcum)
    dec = jnp.exp(btot)
    ri, ci = _iota2(W, W)
    same_head = (ri // HEAD_DIM) == (ci // HEAD_DIM)
    seg = jnp.where(same_head, 1.0, 0.0).astype(BF16)

    heads = []
    for h in range(nh):
        hs = slice(h * HEAD_DIM, (h + 1) * HEAD_DIM)
        scores = jnp.where(incl, _dot_nt(q_in[:, hs], k_in[:, hs]), 0.0)
        heads.append(_dot(scores, v[:, hs]))
    yield
    st = state_ref[...]
    parts = []
    for c in range(nc):
        rs = slice(c * CHUNK, (c + 1) * CHUNK)
        parts.append(_dot(q_in[rs, :], st))
        kv = jnp.where(same_head, _dot(k_end[rs, :].T, v[rs, :]), 0.0)
        st = st * dec[c * CHUNK:c * CHUNK + 8, :].T[:, 0:1] + kv
    state_ref[...] = st
    o = jnp.concatenate(heads, axis=1) + jnp.concatenate(parts, axis=0)

    ms = _dot_sel(o * o, seg) * (1.0 / HEAD_DIM)
    o = o * lax.rsqrt(ms + NORM_EPS) * ng_ref[...]
    y_ref[...] = o * (gg * _sigmoid(gg))


def _rwkv_body(first, cols_ref, vf_ref, mu_ref, w0_ref, w2_ref, a0_ref, a2_ref, g2_ref,
                 kk_ref, ka_ref, rk_ref, lg_ref, lb_ref, v0_ref, v1_ref, v2_ref,
                 y_ref, vout_ref, prev_ref, state_ref):
    L = cols_ref.shape[0]
    W = RWKV_WIDTH
    D = HEAD_DIM
    nh = W // D
    nc = L // CHUNK

    cols = cols_ref[...]
    full = jnp.concatenate([prev_ref[...], cols], axis=0)
    shifted = pltpu.roll(full, 1, axis=0)[8:, :]
    prev_ref[...] = cols[L - 8:, :]
    xx = cols + (shifted - cols) * mu_ref[...]
    r = xx[:, 0:W]
    k = xx[:, W:2 * W]
    v = xx[:, 2 * W:3 * W]
    lr = xx[:, 3 * W:]
    log_w = -_softplus(-(w0_ref[...] + _dot(jnp.tanh(lr), w2_ref[...]))) - 0.5
    lw = -jnp.exp(log_w)
    iclr = _sigmoid(a0_ref[...] + _dot(lr, a2_ref[...]))
    gate = _dot(_sigmoid(lr), g2_ref[...])
    if first:
        vout_ref[...] = v
    else:
        vout_ref[...] = jnp.zeros_like(vout_ref)
        mix = _sigmoid(v0_ref[...] + _dot(_dot(v, v1_ref[...]), v2_ref[...]))
        v = v + (vf_ref[...] - v) * mix

    ri, ci = _iota2(W, W)
    same_head = (ri // D) == (ci // D)
    seg = jnp.where(same_head, 1.0, 0.0).astype(BF16)
    kk = k * kk_ref[...]
    kk = kk / jnp.maximum(jnp.sqrt(_dot_sel(kk * kk, seg)), 1e-12)
    k = k * (1.0 + (iclr - 1.0) * ka_ref[...])
    b = kk * iclr

    _, incl, strict = _chunk_masks(L)
    g = _sel_dot(jnp.where(incl, 1.0, 0.0).astype(BF16), lw)
    gtot = jnp.concatenate(
        [jnp.broadcast_to(g[(c + 1) * CHUNK - 1:(c + 1) * CHUNK, :], (CHUNK, W)) for c in range(nc)],
        axis=0)
    e_neg = jnp.exp(-g)
    e_rem = jnp.exp(gtot - g)
    r_t = r * jnp.exp(g)
    a_t = -kk * jnp.exp(g - lw)
    b_t = b * e_neg
    k_t = k * e_neg
    b_e = b * e_rem
    k_e = k * e_rem
    w_l = jnp.exp(gtot)
    eye = jnp.where(_iota2(L, L)[0] == _iota2(L, L)[1], 1.0, 0.0)

    yield
    hsl = [slice(h * D, (h + 1) * D) for h in range(nh)]
    a_ab, a_rb, akv, arkv = [], [], [], []
    for hs in hsl:
        big = _dot_nt(jnp.concatenate([a_t[:, hs], r_t[:, hs]], axis=0),
                      jnp.concatenate([b_t[:, hs], k_t[:, hs]], axis=0))
        a_ab.append(jnp.where(strict, big[0:L, 0:L], 0.0))
        a_rb.append(jnp.where(incl, big[L:, 0:L], 0.0))
        akv.append(_dot(jnp.where(strict, big[0:L, L:], 0.0), v[:, hs]))
        arkv.append(_dot(jnp.where(incl, big[L:, L:], 0.0), v[:, hs]))
    yield
    tinv = [eye + a for a in a_ab]
    ak = a_ab
    for it in range(CHUNK.bit_length() - 2):
        ak = [_dot(a, a) for a in ak]
        tinv = [t + _dot(t, a) for t, a in zip(tinv, ak)]
        if it % 2 == 1:
            yield
    yield
    qz = [_dot(tinv[h], jnp.concatenate([a_t[:, hs], akv[h]], axis=1)) for h, hs in enumerate(hsl)]
    ry = [_dot(a_rb[h], qz[h]) for h in range(nh)]
    qa = jnp.concatenate([q[:, 0:D] for q in qz], axis=1)
    zf = jnp.concatenate([q[:, D:] for q in qz], axis=1)
    rq = r_t + jnp.concatenate([x[:, 0:D] for x in ry], axis=1)
    yf = jnp.concatenate([x[:, D:] + arkv[h] for h, x in enumerate(ry)], axis=1)

    yield
    p_c, n_c, w_c = [], [], []
    for c in range(nc):
        rs = slice(c * CHUNK, (c + 1) * CHUNK)
        b_tr = b_e[rs, :].T
        k_tr = k_e[rs, :].T
        p_c.append(jnp.where(same_head, _dot(b_tr, qa[rs, :]), 0.0))
        n_c.append(jnp.where(same_head, _dot(jnp.concatenate([b_tr, k_tr], axis=1),
                                             jnp.concatenate([zf[rs, :], v[rs, :]], axis=0)), 0.0))
        w_c.append(w_l[c * CHUNK:c * CHUNK + 8, :].T[:, 0:1])
    st = state_ref[...]
    ys = []
    for c in range(nc):
        rs = slice(c * CHUNK, (c + 1) * CHUNK)
        ys.append(_dot(rq[rs, :], st) + yf[rs, :])
        st = st * w_c[c] + _dot(p_c[c], st) + n_c[c]
    state_ref[...] = st
    y = jnp.concatenate(ys, axis=0)
    yield

    inv = 1.0 / D
    mean = _dot_sel(y, seg) * inv
    yc = y - mean
    var = _dot_sel(yc * yc, seg) * inv
    y = yc * lax.rsqrt(var + RWKV_LNX_EPS) * lg_ref[...] + lb_ref[...]
    y = y + _dot_sel(r * k * rk_ref[...], seg) * v
    y_ref[...] = y * gate


def _post_body(final, h_ref, ys_ref, yg_ref, yr_ref, p_ref, wo_ref, gm_ref, wu_ref, wd_ref,
                 gp_ref, wg_ref, wp_ref, gf_ref, o_ref):
    d_ff = wu_ref.shape[-1]
    d = h_ref.shape[-1]
    o = 0
    mix = None
    for ref in (ys_ref, yg_ref, yr_ref):
        n = ref.shape[-1]
        t = jnp.dot(ref[...].astype(BF16), wo_ref[o:o + n, :], preferred_element_type=F32)
        mix = t if mix is None else mix + t
        o += n
    h = h_ref[...] + mix
    hn = _rms(h, gm_ref[...]).astype(BF16)
    yield
    mlp = None
    for j in range(d_ff // d):
        u = jnp.dot(hn, wu_ref[:, j * d:(j + 1) * d], preferred_element_type=F32)
        u = jnp.square(jnp.maximum(u, 0.0)).astype(BF16)
        t = jnp.dot(u, wd_ref[j * d:(j + 1) * d, :], preferred_element_type=F32)
        mlp = t if mlp is None else mlp + t
        yield
    h = h + mlp
    gate = _sigmoid(jnp.dot(_rms(h, gp_ref[...]).astype(BF16), wg_ref[...],
                            preferred_element_type=F32))
    h = h + gate * jnp.dot(p_ref[...].astype(BF16), wp_ref[...], preferred_element_type=F32)
    if final:
        h = _rms(h, gf_ref[...])
    o_ref[...] = h


N_POST_IN = 13
N_MIX_IN = 5 + 6 + 3 + 14


def _stage_kernel(first, final, with_post, with_mix, *refs):
    o = 0
    if with_post:
        post_in = refs[o:o + N_POST_IN]
        o += N_POST_IN
    if with_mix:
        ssd_in, small, gla_in, rwkv_in, vf = refs[o:o + 5]
        ssd_prm = refs[o + 5:o + 11]
        gla_prm = refs[o + 11:o + 14]
        rwkv_prm = refs[o + 14:o + N_MIX_IN]
        o += N_MIX_IN
    if with_post:
        post_out = refs[o]
        o += 1
    if with_mix:
        y_ssd, y_gla, y_rwkv, vout = refs[o:o + 4]
        scratch = refs[o + 4:]
        tail, ssd_state, gla_state, prev, rwkv_state = scratch

        @pl.when(pl.program_id(1) == 0)
        def _():
            for ref in scratch:
                ref[...] = jnp.zeros_like(ref)

    streams = []
    if with_post:
        streams.append((_post_body(final, *post_in, post_out), 1))
    if with_mix:
        streams.append((itertools.chain(
            _ssd_body(ssd_in, small, *ssd_prm, y_ssd, tail, ssd_state),
            _gla_body(gla_in, small, *gla_prm, y_gla, gla_state),
            _rwkv_body(first, rwkv_in, vf, *rwkv_prm, y_rwkv, vout, prev, rwkv_state)), 2))
    while streams:
        for s in list(streams):
            for _ in range(s[1]):
                if next(s[0], StopIteration) is StopIteration:
                    streams.remove(s)
                    break


def _stage(nb, t, post, mix):
    L = ROW_BLOCK
    W = RWKV_WIDTH
    row = lambda bi, ti: (bi, ti, 0)
    rows = lambda a: pl.BlockSpec((None, L, a.shape[-1]), row)
    dummy = pl.BlockSpec((None, 8, SMALL), lambda bi, ti: (0, 0, 0))
    in_specs, args, out_specs, out_shape, scratch = [], [], [], [], []
    first = final = False
    if post is not None:
        lp, hoff, poff, final = post["layer"], post["hoff"], post["poff"], post["final"]
        h, p = post["h"], post["p"]
        d = h.shape[-1]
        par = lambda bi, ti: (lp, 0, 0)
        wspec = lambda a: pl.BlockSpec((None,) + a.shape[1:], par, pipeline_mode=pl.Buffered(1))
        wo, gm, wu, wd, gp, wg, wp = post["weights"]
        in_specs += [pl.BlockSpec((None, L, d), lambda bi, ti: (bi + hoff, ti, 0)),
                     rows(post["ys"]), rows(post["yg"]), rows(post["yr"]),
                     pl.BlockSpec((None, None, L, p.shape[-1]), lambda bi, ti: (lp, bi + poff, ti, 0)),
                     wspec(wo), wspec(gm), wspec(wu), wspec(wd), wspec(gp), wspec(wg), wspec(wp),
                     pl.BlockSpec((1, d), lambda bi, ti: (0, 0))]
        args += [h, post["ys"], post["yg"], post["yr"], p, wo, gm, wu, wd, gp, wg, wp, post["gf"]]
        out_specs.append(pl.BlockSpec((None, L, d), row))
        out_shape.append(jax.ShapeDtypeStruct((nb, t, d), F32))
    if mix is not None:
        lm = mix["layer"]
        v_first = mix["v_first"]
        first = v_first is None
        c_ssd, c_gla, c_rwkv, c_small = mix["cols"]
        if first:
            v_first = c_rwkv
        parm = lambda bi, ti: (lm, 0, 0)
        parv = lambda bi, ti: (max(lm - 1, 0), 0, 0)
        prm = lambda a, im=parm: pl.BlockSpec((None,) + a.shape[1:], im)
        rwkv_prm = mix["rwkv_prm"]
        mix_prm = list(mix["ssd_prm"]) + list(mix["gla_prm"]) + list(rwkv_prm)
        assert 5 + len(mix_prm) == N_MIX_IN
        in_specs += ([rows(c_ssd), rows(c_small), rows(c_gla), rows(c_rwkv),
                      dummy if first else rows(v_first)]
                     + [prm(a) for a in mix_prm[:-3]] + [prm(a, parv) for a in mix_prm[-3:]])
        args += [c_ssd, c_small, c_gla, c_rwkv, v_first] + mix_prm
        out_specs += [pl.BlockSpec((None, L, SSD_WIDTH), row), pl.BlockSpec((None, L, GLA_WIDTH), row),
                      pl.BlockSpec((None, L, W), row),
                      pl.BlockSpec((None, L, W), row) if first else dummy]
        out_shape += [jax.ShapeDtypeStruct((nb, t, SSD_WIDTH), F32),
                      jax.ShapeDtypeStruct((nb, t, GLA_WIDTH), F32),
                      jax.ShapeDtypeStruct((nb, t, W), F32),
                      jax.ShapeDtypeStruct((nb, t, W), F32) if first
                      else jax.ShapeDtypeStruct((1, 8, SMALL), F32)]
        scratch = [
            pltpu.VMEM((8, SSD_XBC), F32),
            pltpu.VMEM((SSD_GROUPS, SSD_STATE, SSD_WIDTH // SSD_GROUPS), F32),
            pltpu.VMEM((GLA_WIDTH, GLA_WIDTH), F32),
            pltpu.VMEM((8, RWKV_COLS), F32),
            pltpu.VMEM((W, W), F32),
        ]
    return pl.pallas_call(
        functools.partial(_stage_kernel, first, final, post is not None, mix is not None),
        grid=(nb, t // L),
        in_specs=in_specs,
        out_specs=out_specs,
        out_shape=out_shape,
        scratch_shapes=scratch,
        compiler_params=pltpu.CompilerParams(
            dimension_semantics=("parallel", "arbitrary"), vmem_limit_bytes=VMEM_LIMIT),
        name="stage",
    )(*args)


def _rows(a):
    return a.reshape(a.shape[0], 1, -1)


def _pad_to(a, axis, n, offset=0):
    pads = [(0, 0)] * a.ndim
    pads[axis] = (offset, n - a.shape[axis] - offset)
    return jnp.pad(a, pads)


def kernel(x, p, norm_mix_g, w_in, ssd_conv_w, ssd_conv_b, ssd_dt_bias, ssd_a_log, ssd_d, ssd_norm_g, gla_alpha_w, gla_alpha_b, gla_norm_g, rwkv_mu, rwkv_w0, rwkv_w2, rwkv_a0, rwkv_a2, rwkv_g2, rwkv_k_k, rwkv_k_a, rwkv_r_k, rwkv_lnx_g, rwkv_lnx_b, rwkv_v0, rwkv_v1, rwkv_v2, w_out, norm_mlp_g, w_up, w_down, norm_ple_g, w_ple_gate, w_ple_proj, norm_final_g):
    depth = w_in.shape[0]
    assert x.shape[1] % ROW_BLOCK == 0 and ROW_BLOCK % CHUNK == 0

    ssd_cols = SSD_WIDTH + SSD_XBC + SSD_HEADS
    gla_cols = 4 * GLA_WIDTH + GLA_GATE_RANK
    s0, g0, r0 = 0, ssd_cols, ssd_cols + gla_cols
    small_w = jnp.concatenate(
        [w_in[:, :, s0 + SSD_WIDTH + SSD_XBC:g0], w_in[:, :, g0 + 4 * GLA_WIDTH:r0]], axis=-1)
    w_in_p = jnp.concatenate(
        [w_in[:, :, s0:s0 + SSD_WIDTH + SSD_XBC], w_in[:, :, g0:g0 + 4 * GLA_WIDTH],
         w_in[:, :, r0:], _pad_to(small_w, 2, SMALL)], axis=-1).astype(BF16)

    dtb = _rows(_pad_to(ssd_dt_bias, 1, SMALL))
    aneg = _rows(_pad_to(-jnp.exp(ssd_a_log), 1, SMALL))
    dskip = _rows(jnp.repeat(ssd_d, HEAD_DIM, axis=1))
    gla_aw = _pad_to(gla_alpha_w, 1, SMALL, offset=SSD_HEADS)
    gla_ng = _rows(jnp.tile(gla_norm_g, (1, GLA_WIDTH // HEAD_DIM)))
    w2p = _pad_to(rwkv_w2, 1, RWKV_LR, offset=0)
    a2p = _pad_to(rwkv_a2, 1, RWKV_LR, offset=RWKV_DECAY_RANK)
    g2p = _pad_to(rwkv_g2, 1, RWKV_LR, offset=RWKV_DECAY_RANK + RWKV_ICLR_RANK)
    v1p = _pad_to(rwkv_v1, 2, SMALL)
    v2p = _pad_to(rwkv_v2, 1, SMALL)
    rwkv_prm = (_rows(rwkv_mu), _rows(rwkv_w0), w2p, _rows(rwkv_a0), a2p, g2p,
                _rows(rwkv_k_k), _rows(rwkv_k_a), _rows(rwkv_r_k.reshape(depth, -1)),
                _rows(rwkv_lnx_g), _rows(rwkv_lnx_b), _rows(rwkv_v0), v1p, v2p)

    wo = w_out.astype(BF16)
    wu = w_up.astype(BF16)
    wd = w_down.astype(BF16)
    wg = w_ple_gate.astype(BF16)
    wp = w_ple_proj.astype(BF16)
    gmix, gmlp, gple = _rows(norm_mix_g), _rows(norm_mlp_g), _rows(norm_ple_g)
    gfin = norm_final_g.reshape(1, -1)
    cb = _rows(ssd_conv_b)
    ssd_ng = _rows(ssd_norm_g)
    gla_ab = _rows(gla_alpha_b)

    nb_all, t, _ = x.shape
    assert nb_all % 2 == 0
    nb = nb_all // 2
    weights = (wo, gmlp, wu, wd, gple, wg, wp)
    ssd_prm = (ssd_conv_w, cb, dtb, aneg, dskip, ssd_ng)
    gla_prm = (gla_aw, gla_ab, gla_ng)
    h = [x, x]
    hoff = [0, nb]
    v_first = [None, None]
    pending = None
    for i in range(depth):
        for g in (0, 1):
            cols = _in_proj(h[g], hoff[g], nb, gmix, w_in_p, i)
            mix = dict(cols=cols, v_first=v_first[g], ssd_prm=ssd_prm, gla_prm=gla_prm,
                       rwkv_prm=rwkv_prm, layer=i)
            outs = _stage(nb, t, pending, mix)
            if pending is not None:
                og = 1 - g
                h[og], hoff[og] = outs[0], 0
                outs = outs[1:]
            y_ssd, y_gla, y_rwkv, vout = outs
            if i == 0:
                v_first[g] = vout
            pending = dict(h=h[g], hoff=hoff[g], ys=y_ssd, yg=y_gla, yr=y_rwkv, p=p,
                           poff=g * nb, weights=weights, gf=gfin, layer=i, final=(i == depth - 1))
    h[1] = _stage(nb, t, pending, None)[0]
    return jnp.concatenate(h, axis=0)
```

```python
import functools
import itertools

import jax
import jax.numpy as jnp
from jax import lax
from jax.experimental import pallas as pl
from jax.experimental.pallas import tpu as pltpu

F32 = jnp.float32
BF16 = jnp.bfloat16

NORM_EPS = 1e-6
HEAD_DIM = 64
SSD_HEADS = 8
SSD_GROUPS = 2
SSD_STATE = 128
SSD_WIDTH = SSD_HEADS * HEAD_DIM
SSD_XBC = SSD_WIDTH + 2 * SSD_GROUPS * SSD_STATE
SSD_CONV = 4
GLA_WIDTH = 4 * HEAD_DIM
GLA_GATE_RANK = 16
GLA_GATE_NORMALIZER = 16.0
RWKV_WIDTH = 4 * HEAD_DIM
RWKV_DECAY_RANK = 32
RWKV_ICLR_RANK = 32
RWKV_GATE_RANK = 64
RWKV_LR = RWKV_DECAY_RANK + RWKV_ICLR_RANK + RWKV_GATE_RANK
RWKV_COLS = 3 * RWKV_WIDTH + RWKV_LR
RWKV_LNX_EPS = 64e-5
SMALL = 128

CHUNK = 64
ROW_BLOCK = 256
VMEM_LIMIT = 56 * 1024 * 1024


def _dot(a, b):
    return jnp.dot(a.astype(BF16), b.astype(BF16), preferred_element_type=F32)


def _dot_nt(a, b):
    return lax.dot_general(a.astype(BF16), b.astype(BF16), (((1,), (1,)), ((), ())),
                           preferred_element_type=F32)


def _split(a, pieces):
    out = []
    r = a
    for _ in range(pieces):
        p = r.astype(BF16)
        out.append(p)
        r = r - p.astype(F32)
    return out


def _sel_dot(m01, a, pieces=2):
    acc = None
    for p in _split(a, pieces):
        t = jnp.dot(m01, p, preferred_element_type=F32)
        acc = t if acc is None else acc + t
    return acc


def _dot_sel(a, m01, pieces=1):
    acc = None
    for p in _split(a, pieces):
        t = jnp.dot(p, m01, preferred_element_type=F32)
        acc = t if acc is None else acc + t
    return acc


def _iota2(n, m):
    return (lax.broadcasted_iota(jnp.int32, (n, m), 0),
            lax.broadcasted_iota(jnp.int32, (n, m), 1))


def _softplus(x):
    return jnp.maximum(x, 0.0) + jnp.log1p(jnp.exp(-jnp.abs(x)))


def _sigmoid(x):
    return 1.0 / (1.0 + jnp.exp(-x))


def _rms(x, g):
    return x * lax.rsqrt(jnp.mean(x * x, axis=-1, keepdims=True) + NORM_EPS) * g


IN_WIDTHS = (SSD_WIDTH + SSD_XBC, 4 * GLA_WIDTH, RWKV_COLS, SMALL)


def _in_proj(h_ref, g_ref, w_ref):
    xn = _rms(h_ref[...], g_ref[...]).astype(BF16)
    cols, o = [], 0
    for n in IN_WIDTHS:
        cols.append(jnp.dot(xn, w_ref[:, o:o + n], preferred_element_type=F32))
        o += n
    return cols


def _ssd_body(main_ref, small_ref, cw_ref, cb_ref, dtb_ref, aneg_ref, dskip_ref, ng_ref,
              y_ref, tail_ref, state_ref):
    L = main_ref.shape[0]
    hg = SSD_HEADS // SSD_GROUPS

    z = main_ref[:, 0:SSD_WIDTH]
    u = main_ref[:, SSD_WIDTH:SSD_WIDTH + SSD_XBC]
    full = jnp.concatenate([tail_ref[...], u], axis=0)
    acc = u * cw_ref[SSD_CONV - 1:SSD_CONV, :] + cb_ref[...]
    for j in range(1, SSD_CONV):
        acc = acc + pltpu.roll(full, j, axis=0)[8:, :] * cw_ref[SSD_CONV - 1 - j:SSD_CONV - j, :]
    tail_ref[...] = u[L - 8:, :]
    xbc = acc * _sigmoid(acc)
    yield
    xs = xbc[:, 0:SSD_WIDTH]
    bm = xbc[:, SSD_WIDTH:SSD_WIDTH + SSD_GROUPS * SSD_STATE]
    cm = xbc[:, SSD_WIDTH + SSD_GROUPS * SSD_STATE:]

    dt = _softplus(small_ref[...] + dtb_ref[...])
    da = dt * aneg_ref[...]
    ri, ci = _iota2(L, L)
    causal = ri >= ci
    tril = jnp.where(causal, 1.0, 0.0).astype(BF16)
    cs = _sel_dot(tril, da, pieces=3)
    cs_t = cs.T
    cs_last = cs[L - 1:L, :]
    e_cs = jnp.exp(cs)
    e_rem = jnp.exp(cs_last - cs)
    e_last = jnp.exp(cs_last)
    yield

    ys = []
    for g in range(SSD_GROUPS):
        bg_t = bm[:, g * SSD_STATE:(g + 1) * SSD_STATE].T
        cg = cm[:, g * SSD_STATE:(g + 1) * SSD_STATE]
        cb = _dot(cg, bg_t)
        st = state_ref[g]
        y_off = _dot(cg, st)
        xws, decs = [], []
        for hh in range(hg):
            h = g * hg + hh
            seg = cs[:, h:h + 1] - cs_t[h:h + 1, :]
            lmat = jnp.exp(jnp.where(causal, seg, -jnp.inf))
            xdt = xs[:, h * HEAD_DIM:(h + 1) * HEAD_DIM] * dt[:, h:h + 1]
            y_h = _dot(cb * lmat, xdt) + y_off[:, hh * HEAD_DIM:(hh + 1) * HEAD_DIM] * e_cs[:, h:h + 1]
            ys.append(y_h)
            xws.append(xdt * e_rem[:, h:h + 1])
            decs.append(jnp.broadcast_to(e_last[:, h:h + 1], (1, HEAD_DIM)))
            if hh % 2 == 1:
                yield
        inc = _dot(bg_t, jnp.concatenate(xws, axis=1))
        state_ref[g] = st * jnp.concatenate(decs, axis=1) + inc
        yield

    y = jnp.concatenate(ys, axis=1) + xs * dskip_ref[...]
    y = y * (z * _sigmoid(z))
    gw = SSD_WIDTH // SSD_GROUPS
    for g in range(SSD_GROUPS):
        sl = slice(g * gw, (g + 1) * gw)
        y_ref[:, sl] = _rms(y[:, sl], ng_ref[:, sl])


def _chunk_masks(L):
    ri, ci = _iota2(L, L)
    same = (ri // CHUNK) == (ci // CHUNK)
    return same, same & (ri >= ci), same & (ri > ci)


def _gla_body(main_ref, small_ref, aw_ref, ab_ref, ng_ref, y_ref, state_ref):
    L = main_ref.shape[0]
    nh = GLA_WIDTH // HEAD_DIM
    W = GLA_WIDTH

    q = main_ref[:, 0:W] * (HEAD_DIM ** -0.5)
    k = main_ref[:, W:2 * W]
    v = main_ref[:, 2 * W:3 * W]
    gg = main_ref[:, 3 * W:4 * W]
    x = _dot(small_ref[...], aw_ref[...]) + ab_ref[...]
    la = -_softplus(-x) * (1.0 / GLA_GATE_NORMALIZER)
    _, incl, _ = _chunk_masks(L)
    nc = L // CHUNK
    bcum = _sel_dot(jnp.where(incl, 1.0, 0.0).astype(BF16), la)
    btot = jnp.concatenate(
        [jnp.broadcast_to(bcum[(c + 1) * CHUNK - 1:(c + 1) * CHUNK, :], (CHUNK, W)) for c in range(nc)],
        axis=0)
    yield
    q_in = q * jnp.exp(bcum)
    k_in = k * jnp.exp(-bcum)
    k_end = k * jnp.exp(btot - bcum)
    dec = jnp.exp(btot)
    ri, ci = _iota2(W, W)
    same_head = (ri // HEAD_DIM) == (ci // HEAD_DIM)
    seg = jnp.where(same_head, 1.0, 0.0).astype(BF16)

    heads = []
    for h in range(nh):
        hs = slice(h * HEAD_DIM, (h + 1) * HEAD_DIM)
        scores = jnp.where(incl, _dot_nt(q_in[:, hs], k_in[:, hs]), 0.0)
        heads.append(_dot(scores, v[:, hs]))
        if h % 2 == 0:
            yield
    yield
    st = state_ref[...]
    parts = []
    for c in range(nc):
        rs = slice(c * CHUNK, (c + 1) * CHUNK)
        parts.append(_dot(q_in[rs, :], st))
        kv = jnp.where(same_head, _dot(k_end[rs, :].T, v[rs, :]), 0.0)
        st = st * dec[c * CHUNK:c * CHUNK + 8, :].T[:, 0:1] + kv
    state_ref[...] = st
    o = jnp.concatenate(heads, axis=1) + jnp.concatenate(parts, axis=0)

    ms = _dot_sel(o * o, seg) * (1.0 / HEAD_DIM)
    o = o * lax.rsqrt(ms + NORM_EPS) * ng_ref[...]
    y_ref[...] = o * (gg * _sigmoid(gg))


def _rwkv_body(first, cols_ref, vf_ref, mu_ref, w0_ref, w2_ref, a0_ref, a2_ref, g2_ref,
                 kk_ref, ka_ref, rk_ref, lg_ref, lb_ref, v0_ref, v1_ref, v2_ref,
                 y_ref, vout_ref, prev_ref, state_ref):
    L = cols_ref.shape[0]
    W = RWKV_WIDTH
    D = HEAD_DIM
    nh = W // D
    nc = L // CHUNK

    cols = cols_ref[...]
    full = jnp.concatenate([prev_ref[...], cols], axis=0)
    shifted = pltpu.roll(full, 1, axis=0)[8:, :]
    prev_ref[...] = cols[L - 8:, :]
    xx = cols + (shifted - cols) * mu_ref[...]
    r = xx[:, 0:W]
    k = xx[:, W:2 * W]
    v = xx[:, 2 * W:3 * W]
    lr = xx[:, 3 * W:]
    log_w = -_softplus(-(w0_ref[...] + _dot(jnp.tanh(lr), w2_ref[...]))) - 0.5
    lw = -jnp.exp(log_w)
    iclr = _sigmoid(a0_ref[...] + _dot(lr, a2_ref[...]))
    gate = _dot(_sigmoid(lr), g2_ref[...])
    if first:
        vout_ref[...] = v
    else:
        vout_ref[...] = jnp.zeros_like(vout_ref)
        mix = _sigmoid(v0_ref[...] + _dot(_dot(v, v1_ref[...]), v2_ref[...]))
        v = v + (vf_ref[...] - v) * mix

    yield
    ri, ci = _iota2(W, W)
    same_head = (ri // D) == (ci // D)
    seg = jnp.where(same_head, 1.0, 0.0).astype(BF16)
    kk = k * kk_ref[...]
    kk = kk / jnp.maximum(jnp.sqrt(_dot_sel(kk * kk, seg)), 1e-12)
    k = k * (1.0 + (iclr - 1.0) * ka_ref[...])
    b = kk * iclr

    _, incl, strict = _chunk_masks(L)
    g = _sel_dot(jnp.where(incl, 1.0, 0.0).astype(BF16), lw)
    gtot = jnp.concatenate(
        [jnp.broadcast_to(g[(c + 1) * CHUNK - 1:(c + 1) * CHUNK, :], (CHUNK, W)) for c in range(nc)],
        axis=0)
    e_neg = jnp.exp(-g)
    e_rem = jnp.exp(gtot - g)
    r_t = r * jnp.exp(g)
    a_t = -kk * jnp.exp(g - lw)
    b_t = b * e_neg
    k_t = k * e_neg
    b_e = b * e_rem
    k_e = k * e_rem
    w_l = jnp.exp(gtot)
    eye = jnp.where(_iota2(L, L)[0] == _iota2(L, L)[1], 1.0, 0.0)

    yield
    hsl = [slice(h * D, (h + 1) * D) for h in range(nh)]
    a_ab, a_rb, akv, arkv = [], [], [], []
    for hs in hsl:
        big = _dot_nt(jnp.concatenate([a_t[:, hs], r_t[:, hs]], axis=0),
                      jnp.concatenate([b_t[:, hs], k_t[:, hs]], axis=0))
        a_ab.append(jnp.where(strict, big[0:L, 0:L], 0.0))
        a_rb.append(jnp.where(incl, big[L:, 0:L], 0.0))
        akv.append(_dot(jnp.where(strict, big[0:L, L:], 0.0), v[:, hs]))
        arkv.append(_dot(jnp.where(incl, big[L:, L:], 0.0), v[:, hs]))
        yield
    tinv = [eye + a for a in a_ab]
    ak = a_ab
    for _ in range(CHUNK.bit_length() - 2):
        ak = [_dot(a, a) for a in ak]
        tinv = [t + _dot(t, a) for t, a in zip(tinv, ak)]
        yield
    qz = [_dot(tinv[h], jnp.concatenate([a_t[:, hs], akv[h]], axis=1)) for h, hs in enumerate(hsl)]
    ry = [_dot(a_rb[h], qz[h]) for h in range(nh)]
    qa = jnp.concatenate([q[:, 0:D] for q in qz], axis=1)
    zf = jnp.concatenate([q[:, D:] for q in qz], axis=1)
    rq = r_t + jnp.concatenate([x[:, 0:D] for x in ry], axis=1)
    yf = jnp.concatenate([x[:, D:] + arkv[h] for h, x in enumerate(ry)], axis=1)

    yield
    p_c, n_c, w_c = [], [], []
    for c in range(nc):
        rs = slice(c * CHUNK, (c + 1) * CHUNK)
        b_tr = b_e[rs, :].T
        k_tr = k_e[rs, :].T
        p_c.append(jnp.where(same_head, _dot(b_tr, qa[rs, :]), 0.0))
        n_c.append(jnp.where(same_head, _dot(jnp.concatenate([b_tr, k_tr], axis=1),
                                             jnp.concatenate([zf[rs, :], v[rs, :]], axis=0)), 0.0))
        w_c.append(w_l[c * CHUNK:c * CHUNK + 8, :].T[:, 0:1])
        if c % 2 == 1:
            yield
    st = state_ref[...]
    ys = []
    for c in range(nc):
        rs = slice(c * CHUNK, (c + 1) * CHUNK)
        ys.append(_dot(rq[rs, :], st) + yf[rs, :])
        st = st * w_c[c] + _dot(p_c[c], st) + n_c[c]
    state_ref[...] = st
    y = jnp.concatenate(ys, axis=0)
    yield

    inv = 1.0 / D
    mean = _dot_sel(y, seg) * inv
    yc = y - mean
    var = _dot_sel(yc * yc, seg) * inv
    y = yc * lax.rsqrt(var + RWKV_LNX_EPS) * lg_ref[...] + lb_ref[...]
    y = y + _dot_sel(r * k * rk_ref[...], seg) * v
    y_ref[...] = y * gate


def _post_body(final, h_ref, ys_ref, yg_ref, yr_ref, p_ref, wo_ref, gm_ref, wu_ref, wd_ref,
                 gp_ref, wg_ref, wp_ref, gf_ref, o_ref):
    d_ff = wu_ref.shape[-1]
    d = h_ref.shape[-1]
    o = 0
    mix = None
    for ref in (ys_ref, yg_ref, yr_ref):
        n = ref.shape[-1]
        t = jnp.dot(ref[...].astype(BF16), wo_ref[o:o + n, :], preferred_element_type=F32)
        mix = t if mix is None else mix + t
        o += n
    h = h_ref[...] + mix
    hn = _rms(h, gm_ref[...]).astype(BF16)
    yield
    mlp = None
    for j in range(d_ff // d):
        u = jnp.dot(hn, wu_ref[:, j * d:(j + 1) * d], preferred_element_type=F32)
        u = jnp.square(jnp.maximum(u, 0.0)).astype(BF16)
        yield
        t = jnp.dot(u, wd_ref[j * d:(j + 1) * d, :], preferred_element_type=F32)
        mlp = t if mlp is None else mlp + t
        yield
    h = h + mlp
    gate = _sigmoid(jnp.dot(_rms(h, gp_ref[...]).astype(BF16), wg_ref[...],
                            preferred_element_type=F32))
    yield
    h = h + gate * jnp.dot(p_ref[...].astype(BF16), wp_ref[...], preferred_element_type=F32)
    if final:
        h = _rms(h, gf_ref[...])
    o_ref[...] = h


N_POST_IN = 13
N_MIX_IN = 4 + 6 + 3 + 14


def _stage_kernel(first, final, with_post, with_mix, carry, *refs):
    o = 0
    if with_post:
        post_in = refs[o:o + N_POST_IN]
        o += N_POST_IN
        if carry:
            carry_in = refs[o]
            o += 1
    if with_mix:
        h_in, g_in, w_in, vf = refs[o:o + 4]
        ssd_prm = refs[o + 4:o + 10]
        gla_prm = refs[o + 10:o + 13]
        rwkv_prm = refs[o + 13:o + N_MIX_IN]
        o += N_MIX_IN
    if with_post:
        post_out = refs[o]
        o += 1
        if carry:
            post_out[0] = carry_in[...]
            post_out = post_out.at[1]
    if with_mix:
        y_ssd, y_gla, y_rwkv, vout = refs[o:o + 4]
        scratch = refs[o + 4:]
        tail, ssd_state, gla_state, prev, rwkv_state = scratch

        @pl.when(pl.program_id(1) == 0)
        def _():
            for ref in scratch:
                ref[...] = jnp.zeros_like(ref)

    streams = []
    if with_post:
        streams.append((_post_body(final, *post_in, post_out), 1))
    if with_mix:
        def mixers():
            ssd_in, gla_in, rwkv_in, small = _in_proj(h_in, g_in, w_in)
            yield
            yield from _ssd_body(ssd_in, small, *ssd_prm, y_ssd, tail, ssd_state)
            yield from _gla_body(gla_in, small, *gla_prm, y_gla, gla_state)
            yield from _rwkv_body(first, rwkv_in, vf, *rwkv_prm, y_rwkv, vout, prev, rwkv_state)
        streams.append((mixers(), 2))
    while streams:
        for s in list(streams):
            for _ in range(s[1]):
                if next(s[0], StopIteration) is StopIteration:
                    streams.remove(s)
                    break


def _stage(nb, t, post, mix):
    L = ROW_BLOCK
    W = RWKV_WIDTH
    row = lambda bi, ti: (bi, ti, 0)
    rows = lambda a: pl.BlockSpec((None, L, a.shape[-1]), row)
    dummy = pl.BlockSpec((None, 8, SMALL), lambda bi, ti: (0, 0, 0))
    in_specs, args, out_specs, out_shape, scratch = [], [], [], [], []
    first = final = carry = False
    once = pl.Buffered(1)
    if post is not None:
        lp, hoff, poff, final = post["layer"], post["hoff"], post["poff"], post["final"]
        h, p = post["h"], post["p"]
        d = h.shape[-1]
        par = lambda bi, ti: (lp, 0, 0)
        wspec = lambda a: pl.BlockSpec((None,) + a.shape[1:], par, pipeline_mode=once)
        wo, gm, wu, wd, gp, wg, wp = post["weights"]
        in_specs += [pl.BlockSpec((None, L, d), lambda bi, ti: (bi + hoff, ti, 0)),
                     rows(post["ys"]), rows(post["yg"]), rows(post["yr"]),
                     pl.BlockSpec((None, None, L, p.shape[-1]), lambda bi, ti: (lp, bi + poff, ti, 0)),
                     wspec(wo), wspec(gm), wspec(wu), wspec(wd), wspec(gp), wspec(wg), wspec(wp),
                     pl.BlockSpec((1, d), lambda bi, ti: (0, 0))]
        args += [h, post["ys"], post["yg"], post["yr"], p, wo, gm, wu, wd, gp, wg, wp, post["gf"]]
        carry = post.get("carry") is not None
        if carry:
            in_specs.append(rows(post["carry"]))
            args.append(post["carry"])
            out_specs.append(pl.BlockSpec((2, None, L, d), lambda bi, ti: (0, bi, ti, 0)))
            out_shape.append(jax.ShapeDtypeStruct((2, nb, t, d), F32))
        else:
            out_specs.append(pl.BlockSpec((None, L, d), row))
            out_shape.append(jax.ShapeDtypeStruct((nb, t, d), F32))
    if mix is not None:
        lm = mix["layer"]
        v_first = mix["v_first"]
        first = v_first is None
        hm, hmoff = mix["h"], mix["hoff"]
        if first:
            v_first = hm
        parm = lambda bi, ti: (lm, 0, 0)
        parv = lambda bi, ti: (max(lm - 1, 0), 0, 0)
        prm = lambda a, im=parm: pl.BlockSpec((None,) + a.shape[1:], im)
        rwkv_prm = mix["rwkv_prm"]
        mix_prm = list(mix["ssd_prm"]) + list(mix["gla_prm"]) + list(rwkv_prm)
        assert 4 + len(mix_prm) == N_MIX_IN
        w_in = mix["w_in"]
        in_specs += ([pl.BlockSpec((None, L, hm.shape[-1]), lambda bi, ti: (bi + hmoff, ti, 0)),
                      prm(mix["gmix"]),
                      pl.BlockSpec((None,) + w_in.shape[1:], parm, pipeline_mode=once),
                      dummy if first else rows(v_first)]
                     + [prm(a) for a in mix_prm[:-3]] + [prm(a, parv) for a in mix_prm[-3:]])
        args += [hm, mix["gmix"], w_in, v_first] + mix_prm
        out_specs += [pl.BlockSpec((None, L, SSD_WIDTH), row), pl.BlockSpec((None, L, GLA_WIDTH), row),
                      pl.BlockSpec((None, L, W), row),
                      pl.BlockSpec((None, L, W), row) if first else dummy]
        out_shape += [jax.ShapeDtypeStruct((nb, t, SSD_WIDTH), F32),
                      jax.ShapeDtypeStruct((nb, t, GLA_WIDTH), F32),
                      jax.ShapeDtypeStruct((nb, t, W), F32),
                      jax.ShapeDtypeStruct((nb, t, W), F32) if first
                      else jax.ShapeDtypeStruct((1, 8, SMALL), F32)]
        scratch = [
            pltpu.VMEM((8, SSD_XBC), F32),
            pltpu.VMEM((SSD_GROUPS, SSD_STATE, SSD_WIDTH // SSD_GROUPS), F32),
            pltpu.VMEM((GLA_WIDTH, GLA_WIDTH), F32),
            pltpu.VMEM((8, RWKV_COLS), F32),
            pltpu.VMEM((W, W), F32),
        ]
    return pl.pallas_call(
        functools.partial(_stage_kernel, first, final, post is not None, mix is not None, carry),
        grid=(nb, t // L),
        in_specs=in_specs,
        out_specs=out_specs,
        out_shape=out_shape,
        scratch_shapes=scratch,
        compiler_params=pltpu.CompilerParams(
            dimension_semantics=("parallel", "arbitrary"), vmem_limit_bytes=VMEM_LIMIT),
        name="stage",
    )(*args)


def _rows(a):
    return a.reshape(a.shape[0], 1, -1)


def _pad_to(a, axis, n, offset=0):
    pads = [(0, 0)] * a.ndim
    pads[axis] = (offset, n - a.shape[axis] - offset)
    return jnp.pad(a, pads)


def kernel(x, p, norm_mix_g, w_in, ssd_conv_w, ssd_conv_b, ssd_dt_bias, ssd_a_log, ssd_d, ssd_norm_g, gla_alpha_w, gla_alpha_b, gla_norm_g, rwkv_mu, rwkv_w0, rwkv_w2, rwkv_a0, rwkv_a2, rwkv_g2, rwkv_k_k, rwkv_k_a, rwkv_r_k, rwkv_lnx_g, rwkv_lnx_b, rwkv_v0, rwkv_v1, rwkv_v2, w_out, norm_mlp_g, w_up, w_down, norm_ple_g, w_ple_gate, w_ple_proj, norm_final_g):
    depth = w_in.shape[0]
    assert x.shape[1] % ROW_BLOCK == 0 and ROW_BLOCK % CHUNK == 0

    ssd_cols = SSD_WIDTH + SSD_XBC + SSD_HEADS
    gla_cols = 4 * GLA_WIDTH + GLA_GATE_RANK
    s0, g0, r0 = 0, ssd_cols, ssd_cols + gla_cols
    small_w = jnp.concatenate(
        [w_in[:, :, s0 + SSD_WIDTH + SSD_XBC:g0], w_in[:, :, g0 + 4 * GLA_WIDTH:r0]], axis=-1)
    w_in_p = jnp.concatenate(
        [w_in[:, :, s0:s0 + SSD_WIDTH + SSD_XBC], w_in[:, :, g0:g0 + 4 * GLA_WIDTH],
         w_in[:, :, r0:], _pad_to(small_w, 2, SMALL)], axis=-1).astype(BF16)

    dtb = _rows(_pad_to(ssd_dt_bias, 1, SMALL))
    aneg = _rows(_pad_to(-jnp.exp(ssd_a_log), 1, SMALL))
    dskip = _rows(jnp.repeat(ssd_d, HEAD_DIM, axis=1))
    gla_aw = _pad_to(gla_alpha_w, 1, SMALL, offset=SSD_HEADS)
    gla_ng = _rows(jnp.tile(gla_norm_g, (1, GLA_WIDTH // HEAD_DIM)))
    w2p = _pad_to(rwkv_w2, 1, RWKV_LR, offset=0)
    a2p = _pad_to(rwkv_a2, 1, RWKV_LR, offset=RWKV_DECAY_RANK)
    g2p = _pad_to(rwkv_g2, 1, RWKV_LR, offset=RWKV_DECAY_RANK + RWKV_ICLR_RANK)
    v1p = _pad_to(rwkv_v1, 2, SMALL)
    v2p = _pad_to(rwkv_v2, 1, SMALL)
    rwkv_prm = (_rows(rwkv_mu), _rows(rwkv_w0), w2p, _rows(rwkv_a0), a2p, g2p,
                _rows(rwkv_k_k), _rows(rwkv_k_a), _rows(rwkv_r_k.reshape(depth, -1)),
                _rows(rwkv_lnx_g), _rows(rwkv_lnx_b), _rows(rwkv_v0), v1p, v2p)

    wo = w_out.astype(BF16)
    wu = w_up.astype(BF16)
    wd = w_down.astype(BF16)
    wg = w_ple_gate.astype(BF16)
    wp = w_ple_proj.astype(BF16)
    gmix, gmlp, gple = _rows(norm_mix_g), _rows(norm_mlp_g), _rows(norm_ple_g)
    gfin = norm_final_g.reshape(1, -1)
    cb = _rows(ssd_conv_b)
    ssd_ng = _rows(ssd_norm_g)
    gla_ab = _rows(gla_alpha_b)

    nb_all, t, _ = x.shape
    assert nb_all % 2 == 0
    nb = nb_all // 2
    weights = (wo, gmlp, wu, wd, gple, wg, wp)
    ssd_prm = (ssd_conv_w, cb, dtb, aneg, dskip, ssd_ng)
    gla_prm = (gla_aw, gla_ab, gla_ng)
    h = [x, x]
    hoff = [0, nb]
    v_first = [None, None]
    pending = None
    for i in range(depth):
        for g in (0, 1):
            mix = dict(h=h[g], hoff=hoff[g], gmix=gmix, w_in=w_in_p, v_first=v_first[g],
                       ssd_prm=ssd_prm, gla_prm=gla_prm, rwkv_prm=rwkv_prm, layer=i)
            outs = _stage(nb, t, pending, mix)
            if pending is not None:
                og = 1 - g
                h[og], hoff[og] = outs[0], 0
                outs = outs[1:]
            y_ssd, y_gla, y_rwkv, vout = outs
            if i == 0:
                v_first[g] = vout
            pending = dict(h=h[g], hoff=hoff[g], ys=y_ssd, yg=y_gla, yr=y_rwkv, p=p,
                           poff=g * nb, weights=weights, gf=gfin, layer=i, final=(i == depth - 1))
    out = _stage(nb, t, dict(pending, carry=h[0]), None)[0]
    return out.reshape(nb_all, t, -1)
```

```python
import functools

import jax
import jax.numpy as jnp
from jax import lax
from jax.experimental import pallas as pl
from jax.experimental.pallas import tpu as pltpu

F32 = jnp.float32
BF16 = jnp.bfloat16

NORM_EPS = 1e-6
HEAD_DIM = 64
SSD_HEADS = 8
SSD_GROUPS = 2
SSD_STATE = 128
SSD_WIDTH = SSD_HEADS * HEAD_DIM
SSD_XBC = SSD_WIDTH + 2 * SSD_GROUPS * SSD_STATE
SSD_CONV = 4
GLA_WIDTH = 4 * HEAD_DIM
GLA_GATE_RANK = 16
GLA_GATE_NORMALIZER = 16.0
RWKV_WIDTH = 4 * HEAD_DIM
RWKV_DECAY_RANK = 32
RWKV_ICLR_RANK = 32
RWKV_GATE_RANK = 64
RWKV_LR = RWKV_DECAY_RANK + RWKV_ICLR_RANK + RWKV_GATE_RANK
RWKV_COLS = 3 * RWKV_WIDTH + RWKV_LR
RWKV_LNX_EPS = 64e-5
SMALL = 128

CHUNK = 64
ROW_BLOCK = 256
V7X_VMEM_BYTES = 64 * 1024 * 1024
VMEM_LIMIT = V7X_VMEM_BYTES * 7 // 8


def _dot(a, b):
    return jnp.dot(a.astype(BF16), b.astype(BF16), preferred_element_type=F32)


def _dot_nt(a, b):
    return lax.dot_general(a.astype(BF16), b.astype(BF16), (((1,), (1,)), ((), ())),
                           preferred_element_type=F32)


def _split(a, pieces):
    out = []
    r = a
    for _ in range(pieces):
        p = r.astype(BF16)
        out.append(p)
        r = r - p.astype(F32)
    return out


def _sel_dot(m01, a, pieces=2):
    acc = None
    for p in _split(a, pieces):
        t = jnp.dot(m01, p, preferred_element_type=F32)
        acc = t if acc is None else acc + t
    return acc


def _dot_sel(a, m01, pieces=1):
    acc = None
    for p in _split(a, pieces):
        t = jnp.dot(p, m01, preferred_element_type=F32)
        acc = t if acc is None else acc + t
    return acc


def _iota2(n, m):
    return (lax.broadcasted_iota(jnp.int32, (n, m), 0),
            lax.broadcasted_iota(jnp.int32, (n, m), 1))


def _softplus(x):
    return jnp.maximum(x, 0.0) + jnp.log1p(jnp.exp(-jnp.abs(x)))


def _sigmoid(x):
    return 1.0 / (1.0 + jnp.exp(-x))


def _rms(x, g):
    return x * lax.rsqrt(jnp.mean(x * x, axis=-1, keepdims=True) + NORM_EPS) * g


IN_WIDTHS = (SSD_WIDTH + SSD_XBC, 4 * GLA_WIDTH, RWKV_COLS + SMALL)


def _in_proj(h_ref, g_ref, w_ref):
    xn = _rms(h_ref[...], g_ref[...]).astype(BF16)
    cols, o = [], 0
    for n in IN_WIDTHS:
        cols.append(jnp.dot(xn, w_ref[:, o:o + n], preferred_element_type=F32))
        o += n
    ssd, gla, rest = cols
    return ssd, gla, rest[:, 0:RWKV_COLS], rest[:, RWKV_COLS:]


def _ssd_body(main_ref, small_ref, cw_ref, cb_ref, dtb_ref, aneg_ref, dskip_ref, ng_ref,
              y_ref, tail_ref, state_ref):
    L = main_ref.shape[0]
    hg = SSD_HEADS // SSD_GROUPS

    z = main_ref[:, 0:SSD_WIDTH]
    u = main_ref[:, SSD_WIDTH:SSD_WIDTH + SSD_XBC]
    full = jnp.concatenate([tail_ref[...], u], axis=0)
    acc = u * cw_ref[SSD_CONV - 1:SSD_CONV, :] + cb_ref[...]
    for j in range(1, SSD_CONV):
        acc = acc + pltpu.roll(full, j, axis=0)[8:, :] * cw_ref[SSD_CONV - 1 - j:SSD_CONV - j, :]
    tail_ref[...] = u[L - 8:, :]
    xbc = acc * _sigmoid(acc)
    yield
    xs = xbc[:, 0:SSD_WIDTH]
    bm = xbc[:, SSD_WIDTH:SSD_WIDTH + SSD_GROUPS * SSD_STATE]
    cm = xbc[:, SSD_WIDTH + SSD_GROUPS * SSD_STATE:]

    dt = _softplus(small_ref[...] + dtb_ref[...])
    da = dt * aneg_ref[...]
    ri, ci = _iota2(L, L)
    causal = ri >= ci
    tril = jnp.where(causal, 1.0, 0.0).astype(BF16)
    cs = _sel_dot(tril, da, pieces=3)
    cs_t = cs.T
    cs_last = cs[L - 1:L, :]
    e_cs = jnp.exp(cs)
    e_rem = jnp.exp(cs_last - cs)
    e_last = jnp.exp(cs_last)
    yield

    ys = []
    for g in range(SSD_GROUPS):
        bg_t = bm[:, g * SSD_STATE:(g + 1) * SSD_STATE].T
        cg = cm[:, g * SSD_STATE:(g + 1) * SSD_STATE]
        cb = _dot(cg, bg_t)
        st = state_ref[g]
        y_off = _dot(cg, st)
        xws, decs = [], []
        for hh in range(hg):
            h = g * hg + hh
            seg = cs[:, h:h + 1] - cs_t[h:h + 1, :]
            lmat = jnp.exp(jnp.where(causal, seg, -jnp.inf))
            xdt = xs[:, h * HEAD_DIM:(h + 1) * HEAD_DIM] * dt[:, h:h + 1]
            y_h = _dot(cb * lmat, xdt) + y_off[:, hh * HEAD_DIM:(hh + 1) * HEAD_DIM] * e_cs[:, h:h + 1]
            ys.append(y_h)
            xws.append(xdt * e_rem[:, h:h + 1])
            decs.append(jnp.broadcast_to(e_last[:, h:h + 1], (1, HEAD_DIM)))
            if hh % 2 == 1:
                yield
        inc = _dot(bg_t, jnp.concatenate(xws, axis=1))
        state_ref[g] = st * jnp.concatenate(decs, axis=1) + inc
        yield

    y = jnp.concatenate(ys, axis=1) + xs * dskip_ref[...]
    y = y * (z * _sigmoid(z))
    gw = SSD_WIDTH // SSD_GROUPS
    for g in range(SSD_GROUPS):
        sl = slice(g * gw, (g + 1) * gw)
        y_ref[:, sl] = _rms(y[:, sl], ng_ref[:, sl])


def _chunk_masks(L):
    ri, ci = _iota2(L, L)
    same = (ri // CHUNK) == (ci // CHUNK)
    return same, same & (ri >= ci), same & (ri > ci)


def _gla_body(main_ref, small_ref, aw_ref, ab_ref, ng_ref, y_ref, state_ref):
    L = main_ref.shape[0]
    nh = GLA_WIDTH // HEAD_DIM
    W = GLA_WIDTH

    q = main_ref[:, 0:W] * (HEAD_DIM ** -0.5)
    k = main_ref[:, W:2 * W]
    v = main_ref[:, 2 * W:3 * W]
    gg = main_ref[:, 3 * W:4 * W]
    x = _dot(small_ref[...], aw_ref[...]) + ab_ref[...]
    la = -_softplus(-x) * (1.0 / GLA_GATE_NORMALIZER)
    _, incl, _ = _chunk_masks(L)
    nc = L // CHUNK
    bcum = _sel_dot(jnp.where(incl, 1.0, 0.0).astype(BF16), la)
    btot = jnp.concatenate(
        [jnp.broadcast_to(bcum[(c + 1) * CHUNK - 1:(c + 1) * CHUNK, :], (CHUNK, W)) for c in range(nc)],
        axis=0)
    yield
    q_in = q * jnp.exp(bcum)
    k_in = k * jnp.exp(-bcum)
    k_end = k * jnp.exp(btot - bcum)
    dec = jnp.exp(btot)
    ri, ci = _iota2(W, W)
    same_head = (ri // HEAD_DIM) == (ci // HEAD_DIM)
    seg = jnp.where(same_head, 1.0, 0.0).astype(BF16)

    heads = []
    for h in range(nh):
        hs = slice(h * HEAD_DIM, (h + 1) * HEAD_DIM)
        scores = jnp.where(incl, _dot_nt(q_in[:, hs], k_in[:, hs]), 0.0)
        heads.append(_dot(scores, v[:, hs]))
        if h % 2 == 0:
            yield
    yield
    st = state_ref[...]
    parts = []
    for c in range(nc):
        rs = slice(c * CHUNK, (c + 1) * CHUNK)
        parts.append(_dot(q_in[rs, :], st))
        kv = jnp.where(same_head, _dot(k_end[rs, :].T, v[rs, :]), 0.0)
        st = st * dec[c * CHUNK:c * CHUNK + 8, :].T[:, 0:1] + kv
    state_ref[...] = st
    o = jnp.concatenate(heads, axis=1) + jnp.concatenate(parts, axis=0)

    ms = _dot_sel(o * o, seg) * (1.0 / HEAD_DIM)
    o = o * lax.rsqrt(ms + NORM_EPS) * ng_ref[...]
    y_ref[...] = o * (gg * _sigmoid(gg))


def _rwkv_body(first, cols_ref, vf_ref, mu_ref, w0_ref, w2_ref, a0_ref, a2_ref, g2_ref,
                 kk_ref, ka_ref, rk_ref, lg_ref, lb_ref, v0_ref, v1_ref, v2_ref,
                 y_ref, vout_ref, prev_ref, state_ref):
    L = cols_ref.shape[0]
    W = RWKV_WIDTH
    D = HEAD_DIM
    nh = W // D
    nc = L // CHUNK

    cols = cols_ref[...]
    full = jnp.concatenate([prev_ref[...], cols], axis=0)
    shifted = pltpu.roll(full, 1, axis=0)[8:, :]
    prev_ref[...] = cols[L - 8:, :]
    xx = cols + (shifted - cols) * mu_ref[...]
    r = xx[:, 0:W]
    k = xx[:, W:2 * W]
    v = xx[:, 2 * W:3 * W]
    lr = xx[:, 3 * W:]
    log_w = -_softplus(-(w0_ref[...] + _dot(jnp.tanh(lr), w2_ref[...]))) - 0.5
    lw = -jnp.exp(log_w)
    iclr = _sigmoid(a0_ref[...] + _dot(lr, a2_ref[...]))
    gate = _dot(_sigmoid(lr), g2_ref[...])
    if first:
        vout_ref[...] = v
    else:
        vout_ref[...] = jnp.zeros_like(vout_ref)
        mix = _sigmoid(v0_ref[...] + _dot(_dot(v, v1_ref[...]), v2_ref[...]))
        v = v + (vf_ref[...] - v) * mix

    yield
    ri, ci = _iota2(W, W)
    same_head = (ri // D) == (ci // D)
    seg = jnp.where(same_head, 1.0, 0.0).astype(BF16)
    kk = k * kk_ref[...]
    kk = kk / jnp.maximum(jnp.sqrt(_dot_sel(kk * kk, seg)), 1e-12)
    k = k * (1.0 + (iclr - 1.0) * ka_ref[...])
    b = kk * iclr

    _, incl, strict = _chunk_masks(L)
    g = _sel_dot(jnp.where(incl, 1.0, 0.0).astype(BF16), lw)
    gtot = jnp.concatenate(
        [jnp.broadcast_to(g[(c + 1) * CHUNK - 1:(c + 1) * CHUNK, :], (CHUNK, W)) for c in range(nc)],
        axis=0)
    e_neg = jnp.exp(-g)
    e_rem = jnp.exp(gtot - g)
    r_t = r * jnp.exp(g)
    a_t = -kk * jnp.exp(g - lw)
    b_t = b * e_neg
    k_t = k * e_neg
    b_e = b * e_rem
    k_e = k * e_rem
    w_l = jnp.exp(gtot)
    eye = jnp.where(_iota2(L, L)[0] == _iota2(L, L)[1], 1.0, 0.0)

    yield
    hsl = [slice(h * D, (h + 1) * D) for h in range(nh)]
    a_ab, a_rb, akv, arkv = [], [], [], []
    for hs in hsl:
        big = _dot_nt(jnp.concatenate([a_t[:, hs], r_t[:, hs]], axis=0),
                      jnp.concatenate([b_t[:, hs], k_t[:, hs]], axis=0))
        a_ab.append(jnp.where(strict, big[0:L, 0:L], 0.0))
        a_rb.append(jnp.where(incl, big[L:, 0:L], 0.0))
        av = _dot(jnp.concatenate([jnp.where(strict, big[0:L, L:], 0.0),
                                   jnp.where(incl, big[L:, L:], 0.0)], axis=0), v[:, hs])
        akv.append(av[0:L])
        arkv.append(av[L:])
        yield
    tinv = [eye + a for a in a_ab]
    ak = [_dot(a, a) for a in a_ab]
    yield
    for _ in range(CHUNK.bit_length() - 3):
        x = [_dot(a, jnp.concatenate([t, a], axis=1)) for t, a in zip(tinv, ak)]
        tinv = [t + xi[:, 0:L] for t, xi in zip(tinv, x)]
        ak = [xi[:, L:] for xi in x]
        yield
    tinv = [t + _dot(a, t) for t, a in zip(tinv, ak)]
    yield
    qz = [_dot(tinv[h], jnp.concatenate([a_t[:, hs], akv[h]], axis=1)) for h, hs in enumerate(hsl)]
    ry = [_dot(a_rb[h], qz[h]) for h in range(nh)]
    qa = jnp.concatenate([q[:, 0:D] for q in qz], axis=1)
    zf = jnp.concatenate([q[:, D:] for q in qz], axis=1)
    rq = r_t + jnp.concatenate([x[:, 0:D] for x in ry], axis=1)
    yf = jnp.concatenate([x[:, D:] + arkv[h] for h, x in enumerate(ry)], axis=1)

    yield
    p_c, n_c, w_c = [], [], []
    for c in range(nc):
        rs = slice(c * CHUNK, (c + 1) * CHUNK)
        b_tr = b_e[rs, :].T
        k_tr = k_e[rs, :].T
        p_c.append(jnp.where(same_head, _dot(b_tr, qa[rs, :]), 0.0))
        n_c.append(jnp.where(same_head, _dot(jnp.concatenate([b_tr, k_tr], axis=1),
                                             jnp.concatenate([zf[rs, :], v[rs, :]], axis=0)), 0.0))
        w_c.append(w_l[c * CHUNK:c * CHUNK + 8, :].T[:, 0:1])
        if c % 2 == 1:
            yield
    st = state_ref[...]
    ys = []
    for c in range(nc):
        rs = slice(c * CHUNK, (c + 1) * CHUNK)
        ys.append(_dot(rq[rs, :], st) + yf[rs, :])
        st = st * w_c[c] + _dot(p_c[c], st) + n_c[c]
    state_ref[...] = st
    y = jnp.concatenate(ys, axis=0)
    yield

    inv = 1.0 / D
    mean = _dot_sel(y, seg) * inv
    yc = y - mean
    var = _dot_sel(yc * yc, seg) * inv
    y = yc * lax.rsqrt(var + RWKV_LNX_EPS) * lg_ref[...] + lb_ref[...]
    y = y + _dot_sel(r * k * rk_ref[...], seg) * v
    y_ref[...] = y * gate


def _post_body(final, h_ref, ys_ref, yg_ref, yr_ref, p_ref, wo_ref, gm_ref, wu_ref, wd_ref,
                 gp_ref, wg_ref, wp_ref, gf_ref, o_ref):
    d_ff = wu_ref.shape[-1]
    d = h_ref.shape[-1]
    y = jnp.concatenate([ys_ref[...], yg_ref[...], yr_ref[...]], axis=1).astype(BF16)
    mix = jnp.dot(y, wo_ref[...], preferred_element_type=F32)
    h = h_ref[...] + mix
    hn = _rms(h, gm_ref[...]).astype(BF16)
    yield
    us = []
    for j in range(d_ff // d):
        u = jnp.dot(hn, wu_ref[:, j * d:(j + 1) * d], preferred_element_type=F32)
        us.append(jnp.square(jnp.maximum(u, 0.0)).astype(BF16))
        yield
    mlp = jnp.dot(jnp.concatenate(us, axis=1), wd_ref[...], preferred_element_type=F32)
    yield
    h = h + mlp
    gate = _sigmoid(jnp.dot(_rms(h, gp_ref[...]).astype(BF16), wg_ref[...],
                            preferred_element_type=F32))
    yield
    h = h + gate * jnp.dot(p_ref[...].astype(BF16), wp_ref[...], preferred_element_type=F32)
    if final:
        h = _rms(h, gf_ref[...])
    o_ref[...] = h


N_POST_IN = 13
N_MIX_IN = 4 + 6 + 3 + 14


def _stage_kernel(first, final, with_post, with_mix, carry, *refs):
    o = 0
    if with_post:
        post_in = refs[o:o + N_POST_IN]
        o += N_POST_IN
        if carry:
            carry_in = refs[o]
            o += 1
    if with_mix:
        h_in, g_in, w_in, vf = refs[o:o + 4]
        ssd_prm = refs[o + 4:o + 10]
        gla_prm = refs[o + 10:o + 13]
        rwkv_prm = refs[o + 13:o + N_MIX_IN]
        o += N_MIX_IN
    if with_post:
        post_out = refs[o]
        o += 1
        if carry:
            post_out[0] = carry_in[...]
            post_out = post_out.at[1]
    if with_mix:
        y_ssd, y_gla, y_rwkv, vout = refs[o:o + 4]
        scratch = refs[o + 4:]
        tail, ssd_state, gla_state, prev, rwkv_state = scratch

        @pl.when(pl.program_id(1) == 0)
        def _():
            for ref in scratch:
                ref[...] = jnp.zeros_like(ref)

    streams = []
    if with_post:
        streams.append((_post_body(final, *post_in, post_out), 1))
    if with_mix:
        def mixers():
            ssd_in, gla_in, rwkv_in, small = _in_proj(h_in, g_in, w_in)
            yield
            yield from _ssd_body(ssd_in, small, *ssd_prm, y_ssd, tail, ssd_state)
            yield from _gla_body(gla_in, small, *gla_prm, y_gla, gla_state)
            yield from _rwkv_body(first, rwkv_in, vf, *rwkv_prm, y_rwkv, vout, prev, rwkv_state)
        streams.append((mixers(), 2))
    while streams:
        for s in list(streams):
            for _ in range(s[1]):
                if next(s[0], StopIteration) is StopIteration:
                    streams.remove(s)
                    break


def _stage(nb, t, post, mix):
    L = ROW_BLOCK
    W = RWKV_WIDTH
    row = lambda bi, ti: (bi, ti, 0)
    rows = lambda a: pl.BlockSpec((None, L, a.shape[-1]), row)
    dummy = pl.BlockSpec((None, 8, SMALL), lambda bi, ti: (0, 0, 0))
    in_specs, args, out_specs, out_shape, scratch = [], [], [], [], []
    first = final = carry = False
    once = pl.Buffered(1)
    if post is not None:
        lp, hoff, poff, final = post["layer"], post["hoff"], post["poff"], post["final"]
        h, p = post["h"], post["p"]
        d = h.shape[-1]
        par = lambda bi, ti: (lp, 0, 0)
        wspec = lambda a: pl.BlockSpec((None,) + a.shape[1:], par, pipeline_mode=once)
        wo, gm, wu, wd, gp, wg, wp = post["weights"]
        in_specs += [pl.BlockSpec((None, L, d), lambda bi, ti: (bi + hoff, ti, 0)),
                     rows(post["ys"]), rows(post["yg"]), rows(post["yr"]),
                     pl.BlockSpec((None, None, L, p.shape[-1]), lambda bi, ti: (lp, bi + poff, ti, 0)),
                     wspec(wo), wspec(gm), wspec(wu), wspec(wd), wspec(gp), wspec(wg), wspec(wp),
                     pl.BlockSpec((1, d), lambda bi, ti: (0, 0))]
        args += [h, post["ys"], post["yg"], post["yr"], p, wo, gm, wu, wd, gp, wg, wp, post["gf"]]
        carry = post.get("carry") is not None
        if carry:
            in_specs.append(rows(post["carry"]))
            args.append(post["carry"])
            out_specs.append(pl.BlockSpec((2, None, L, d), lambda bi, ti: (0, bi, ti, 0)))
            out_shape.append(jax.ShapeDtypeStruct((2, nb, t, d), F32))
        else:
            out_specs.append(pl.BlockSpec((None, L, d), row))
            out_shape.append(jax.ShapeDtypeStruct((nb, t, d), F32))
    if mix is not None:
        lm = mix["layer"]
        v_first = mix["v_first"]
        first = v_first is None
        hm, hmoff = mix["h"], mix["hoff"]
        if first:
            v_first = hm
        parm = lambda bi, ti: (lm, 0, 0)
        parv = lambda bi, ti: (max(lm - 1, 0), 0, 0)
        prm = lambda a, im=parm: pl.BlockSpec((None,) + a.shape[1:], im)
        rwkv_prm = mix["rwkv_prm"]
        mix_prm = list(mix["ssd_prm"]) + list(mix["gla_prm"]) + list(rwkv_prm)
        assert 4 + len(mix_prm) == N_MIX_IN
        w_in = mix["w_in"]
        in_specs += ([pl.BlockSpec((None, L, hm.shape[-1]), lambda bi, ti: (bi + hmoff, ti, 0)),
                      prm(mix["gmix"]),
                      pl.BlockSpec((None,) + w_in.shape[1:], parm, pipeline_mode=once),
                      dummy if first else rows(v_first)]
                     + [prm(a) for a in mix_prm[:-3]] + [prm(a, parv) for a in mix_prm[-3:]])
        args += [hm, mix["gmix"], w_in, v_first] + mix_prm
        out_specs += [pl.BlockSpec((None, L, SSD_WIDTH), row), pl.BlockSpec((None, L, GLA_WIDTH), row),
                      pl.BlockSpec((None, L, W), row),
                      pl.BlockSpec((None, L, W), row) if first else dummy]
        out_shape += [jax.ShapeDtypeStruct((nb, t, SSD_WIDTH), F32),
                      jax.ShapeDtypeStruct((nb, t, GLA_WIDTH), F32),
                      jax.ShapeDtypeStruct((nb, t, W), F32),
                      jax.ShapeDtypeStruct((nb, t, W), F32) if first
                      else jax.ShapeDtypeStruct((1, 8, SMALL), F32)]
        scratch = [
            pltpu.VMEM((8, SSD_XBC), F32),
            pltpu.VMEM((SSD_GROUPS, SSD_STATE, SSD_WIDTH // SSD_GROUPS), F32),
            pltpu.VMEM((GLA_WIDTH, GLA_WIDTH), F32),
            pltpu.VMEM((8, RWKV_COLS), F32),
            pltpu.VMEM((W, W), F32),
        ]
    return pl.pallas_call(
        functools.partial(_stage_kernel, first, final, post is not None, mix is not None, carry),
        grid=(nb, t // L),
        in_specs=in_specs,
        out_specs=out_specs,
        out_shape=out_shape,
        scratch_shapes=scratch,
        compiler_params=pltpu.CompilerParams(
            dimension_semantics=("parallel", "arbitrary"), vmem_limit_bytes=VMEM_LIMIT),
        name="stage",
    )(*args)


def _rows(a):
    return a.reshape(a.shape[0], 1, -1)


def _pad_to(a, axis, n, offset=0):
    pads = [(0, 0)] * a.ndim
    pads[axis] = (offset, n - a.shape[axis] - offset)
    return jnp.pad(a, pads)


def kernel(x, p, norm_mix_g, w_in, ssd_conv_w, ssd_conv_b, ssd_dt_bias, ssd_a_log, ssd_d, ssd_norm_g, gla_alpha_w, gla_alpha_b, gla_norm_g, rwkv_mu, rwkv_w0, rwkv_w2, rwkv_a0, rwkv_a2, rwkv_g2, rwkv_k_k, rwkv_k_a, rwkv_r_k, rwkv_lnx_g, rwkv_lnx_b, rwkv_v0, rwkv_v1, rwkv_v2, w_out, norm_mlp_g, w_up, w_down, norm_ple_g, w_ple_gate, w_ple_proj, norm_final_g):
    depth = w_in.shape[0]
    assert x.shape[1] % ROW_BLOCK == 0 and ROW_BLOCK % CHUNK == 0

    ssd_cols = SSD_WIDTH + SSD_XBC + SSD_HEADS
    gla_cols = 4 * GLA_WIDTH + GLA_GATE_RANK
    s0, g0, r0 = 0, ssd_cols, ssd_cols + gla_cols
    small_w = jnp.concatenate(
        [w_in[:, :, s0 + SSD_WIDTH + SSD_XBC:g0], w_in[:, :, g0 + 4 * GLA_WIDTH:r0]], axis=-1)
    w_in_p = jnp.concatenate(
        [w_in[:, :, s0:s0 + SSD_WIDTH + SSD_XBC], w_in[:, :, g0:g0 + 4 * GLA_WIDTH],
         w_in[:, :, r0:], _pad_to(small_w, 2, SMALL)], axis=-1).astype(BF16)

    dtb = _rows(_pad_to(ssd_dt_bias, 1, SMALL))
    aneg = _rows(_pad_to(-jnp.exp(ssd_a_log), 1, SMALL))
    dskip = _rows(jnp.repeat(ssd_d, HEAD_DIM, axis=1))
    gla_aw = _pad_to(gla_alpha_w, 1, SMALL, offset=SSD_HEADS)
    gla_ng = _rows(jnp.tile(gla_norm_g, (1, GLA_WIDTH // HEAD_DIM)))
    w2p = _pad_to(rwkv_w2, 1, RWKV_LR, offset=0)
    a2p = _pad_to(rwkv_a2, 1, RWKV_LR, offset=RWKV_DECAY_RANK)
    g2p = _pad_to(rwkv_g2, 1, RWKV_LR, offset=RWKV_DECAY_RANK + RWKV_ICLR_RANK)
    v1p = _pad_to(rwkv_v1, 2, SMALL)
    v2p = _pad_to(rwkv_v2, 1, SMALL)
    rwkv_prm = (_rows(rwkv_mu), _rows(rwkv_w0), w2p, _rows(rwkv_a0), a2p, g2p,
                _rows(rwkv_k_k), _rows(rwkv_k_a), _rows(rwkv_r_k.reshape(depth, -1)),
                _rows(rwkv_lnx_g), _rows(rwkv_lnx_b), _rows(rwkv_v0), v1p, v2p)

    wo = w_out.astype(BF16)
    wu = w_up.astype(BF16)
    wd = w_down.astype(BF16)
    wg = w_ple_gate.astype(BF16)
    wp = w_ple_proj.astype(BF16)
    gmix, gmlp, gple = _rows(norm_mix_g), _rows(norm_mlp_g), _rows(norm_ple_g)
    gfin = norm_final_g.reshape(1, -1)
    cb = _rows(ssd_conv_b)
    ssd_ng = _rows(ssd_norm_g)
    gla_ab = _rows(gla_alpha_b)

    nb_all, t, _ = x.shape
    assert nb_all % 2 == 0
    nb = nb_all // 2
    weights = (wo, gmlp, wu, wd, gple, wg, wp)
    ssd_prm = (ssd_conv_w, cb, dtb, aneg, dskip, ssd_ng)
    gla_prm = (gla_aw, gla_ab, gla_ng)
    h = [x, x]
    hoff = [0, nb]
    v_first = [None, None]
    pending = None
    for i in range(depth):
        for g in (0, 1):
            mix = dict(h=h[g], hoff=hoff[g], gmix=gmix, w_in=w_in_p, v_first=v_first[g],
                       ssd_prm=ssd_prm, gla_prm=gla_prm, rwkv_prm=rwkv_prm, layer=i)
            outs = _stage(nb, t, pending, mix)
            if pending is not None:
                og = 1 - g
                h[og], hoff[og] = outs[0], 0
                outs = outs[1:]
            y_ssd, y_gla, y_rwkv, vout = outs
            if i == 0:
                v_first[g] = vout
            pending = dict(h=h[g], hoff=hoff[g], ys=y_ssd, yg=y_gla, yr=y_rwkv, p=p,
                           poff=g * nb, weights=weights, gf=gfin, layer=i, final=(i == depth - 1))
    out = _stage(nb, t, dict(pending, carry=h[0]), None)[0]
    return out.reshape(nb_all, t, -1)
```

```python
import functools

import jax
import jax.numpy as jnp
from jax import lax
from jax.experimental import pallas as pl
from jax.experimental.pallas import tpu as pltpu

F32 = jnp.float32
BF16 = jnp.bfloat16

NORM_EPS = 1e-6
HEAD_DIM = 64
SSD_HEADS = 8
SSD_GROUPS = 2
SSD_STATE = 128
SSD_WIDTH = SSD_HEADS * HEAD_DIM
SSD_XBC = SSD_WIDTH + 2 * SSD_GROUPS * SSD_STATE
SSD_CONV = 4
GLA_WIDTH = 4 * HEAD_DIM
GLA_GATE_RANK = 16
GLA_GATE_NORMALIZER = 16.0
RWKV_WIDTH = 4 * HEAD_DIM
RWKV_DECAY_RANK = 32
RWKV_ICLR_RANK = 32
RWKV_GATE_RANK = 64
RWKV_LR = RWKV_DECAY_RANK + RWKV_ICLR_RANK + RWKV_GATE_RANK
RWKV_COLS = 3 * RWKV_WIDTH + RWKV_LR
RWKV_LNX_EPS = 64e-5
SMALL = 128

CHUNK = 64
ROW_BLOCK = 256
V7X_VMEM_BYTES = 64 * 1024 * 1024
VMEM_LIMIT = V7X_VMEM_BYTES * 7 // 8


def _dot(a, b):
    return jnp.dot(a.astype(BF16), b.astype(BF16), preferred_element_type=F32)


def _dot_nt(a, b):
    return lax.dot_general(a.astype(BF16), b.astype(BF16), (((1,), (1,)), ((), ())),
                           preferred_element_type=F32)


def _split(a, pieces):
    out = []
    r = a
    for _ in range(pieces):
        p = r.astype(BF16)
        out.append(p)
        r = r - p.astype(F32)
    return out


def _sel_dot(m01, a, pieces=2):
    acc = None
    for p in _split(a, pieces):
        t = jnp.dot(m01, p, preferred_element_type=F32)
        acc = t if acc is None else acc + t
    return acc


def _dot_sel(a, m01, pieces=1):
    acc = None
    for p in _split(a, pieces):
        t = jnp.dot(p, m01, preferred_element_type=F32)
        acc = t if acc is None else acc + t
    return acc


def _iota2(n, m):
    return (lax.broadcasted_iota(jnp.int32, (n, m), 0),
            lax.broadcasted_iota(jnp.int32, (n, m), 1))


def _softplus(x):
    return jnp.maximum(x, 0.0) + jnp.log1p(jnp.exp(-jnp.abs(x)))


def _sigmoid(x):
    return 1.0 / (1.0 + jnp.exp(-x))


def _rms(x, g):
    return x * lax.rsqrt(jnp.mean(x * x, axis=-1, keepdims=True) + NORM_EPS) * g


IN_WIDTHS = (SSD_WIDTH + SSD_XBC, 4 * GLA_WIDTH, RWKV_COLS + SMALL)


def _in_proj(h_ref, g_ref, w_ref):
    xn = _rms(h_ref[...], g_ref[...]).astype(BF16)
    cols, o = [], 0
    for n in IN_WIDTHS:
        cols.append(jnp.dot(xn, w_ref[:, o:o + n], preferred_element_type=F32))
        o += n
    ssd, gla, rest = cols
    return ssd, gla, rest[:, 0:RWKV_COLS], rest[:, RWKV_COLS:]


def _ssd_body(main_ref, small_ref, cw_ref, cb_ref, dtb_ref, aneg_ref, dskip_ref, ng_ref,
              y_ref, tail_ref, state_ref):
    L = main_ref.shape[0]
    hg = SSD_HEADS // SSD_GROUPS

    z = main_ref[:, 0:SSD_WIDTH]
    u = main_ref[:, SSD_WIDTH:SSD_WIDTH + SSD_XBC]
    full = jnp.concatenate([tail_ref[...], u], axis=0)
    acc = u * cw_ref[SSD_CONV - 1:SSD_CONV, :] + cb_ref[...]
    for j in range(1, SSD_CONV):
        acc = acc + pltpu.roll(full, j, axis=0)[8:, :] * cw_ref[SSD_CONV - 1 - j:SSD_CONV - j, :]
    tail_ref[...] = u[L - 8:, :]
    xbc = acc * _sigmoid(acc)
    yield
    xs = xbc[:, 0:SSD_WIDTH]
    bm = xbc[:, SSD_WIDTH:SSD_WIDTH + SSD_GROUPS * SSD_STATE]
    cm = xbc[:, SSD_WIDTH + SSD_GROUPS * SSD_STATE:]

    dt = _softplus(small_ref[...] + dtb_ref[...])
    da = dt * aneg_ref[...]
    ri, ci = _iota2(L, L)
    causal = ri >= ci
    tril = jnp.where(causal, 1.0, 0.0).astype(BF16)
    cs = _sel_dot(tril, da, pieces=3)
    cs_t = cs.T
    cs_last = cs[L - 1:L, :]
    e_cs = jnp.exp(cs)
    e_rem = jnp.exp(cs_last - cs)
    e_last = jnp.exp(cs_last)
    yield

    ys = []
    for g in range(SSD_GROUPS):
        bg_t = bm[:, g * SSD_STATE:(g + 1) * SSD_STATE].T
        cg = cm[:, g * SSD_STATE:(g + 1) * SSD_STATE]
        cb = _dot(cg, bg_t)
        st = state_ref[g]
        y_off = _dot(cg, st)
        xws, decs = [], []
        for hh in range(hg):
            h = g * hg + hh
            seg = cs[:, h:h + 1] - cs_t[h:h + 1, :]
            lmat = jnp.exp(jnp.where(causal, seg, -jnp.inf))
            xdt = xs[:, h * HEAD_DIM:(h + 1) * HEAD_DIM] * dt[:, h:h + 1]
            y_h = _dot(cb * lmat, xdt) + y_off[:, hh * HEAD_DIM:(hh + 1) * HEAD_DIM] * e_cs[:, h:h + 1]
            ys.append(y_h)
            xws.append(xdt * e_rem[:, h:h + 1])
            decs.append(jnp.broadcast_to(e_last[:, h:h + 1], (1, HEAD_DIM)))
            if hh % 2 == 1:
                yield
        inc = _dot(bg_t, jnp.concatenate(xws, axis=1))
        state_ref[g] = st * jnp.concatenate(decs, axis=1) + inc
        yield

    y = jnp.concatenate(ys, axis=1) + xs * dskip_ref[...]
    y = y * (z * _sigmoid(z))
    gw = SSD_WIDTH // SSD_GROUPS
    for g in range(SSD_GROUPS):
        sl = slice(g * gw, (g + 1) * gw)
        y_ref[:, sl] = _rms(y[:, sl], ng_ref[:, sl])


def _chunk_masks(L):
    ri, ci = _iota2(L, L)
    same = (ri // CHUNK) == (ci // CHUNK)
    return same, same & (ri >= ci), same & (ri > ci)


def _gla_body(main_ref, small_ref, aw_ref, ab_ref, ng_ref, y_ref, state_ref):
    L = main_ref.shape[0]
    nh = GLA_WIDTH // HEAD_DIM
    W = GLA_WIDTH

    q = main_ref[:, 0:W] * (HEAD_DIM ** -0.5)
    k = main_ref[:, W:2 * W]
    v = main_ref[:, 2 * W:3 * W]
    gg = main_ref[:, 3 * W:4 * W]
    x = _dot(small_ref[...], aw_ref[...]) + ab_ref[...]
    la = -_softplus(-x) * (1.0 / GLA_GATE_NORMALIZER)
    _, incl, _ = _chunk_masks(L)
    nc = L // CHUNK
    bcum = _sel_dot(jnp.where(incl, 1.0, 0.0).astype(BF16), la)
    btot = jnp.concatenate(
        [jnp.broadcast_to(bcum[(c + 1) * CHUNK - 1:(c + 1) * CHUNK, :], (CHUNK, W)) for c in range(nc)],
        axis=0)
    yield
    q_in = q * jnp.exp(bcum)
    k_in = k * jnp.exp(-bcum)
    k_end = k * jnp.exp(btot - bcum)
    dec = jnp.exp(btot)
    ri, ci = _iota2(W, W)
    same_head = (ri // HEAD_DIM) == (ci // HEAD_DIM)
    seg = jnp.where(same_head, 1.0, 0.0).astype(BF16)

    heads = []
    for h in range(nh):
        hs = slice(h * HEAD_DIM, (h + 1) * HEAD_DIM)
        scores = jnp.where(incl, _dot_nt(q_in[:, hs], k_in[:, hs]), 0.0)
        heads.append(_dot(scores, v[:, hs]))
        if h % 2 == 0:
            yield
    yield
    st = state_ref[...]
    parts = []
    for c in range(nc):
        rs = slice(c * CHUNK, (c + 1) * CHUNK)
        parts.append(_dot(q_in[rs, :], st))
        kv = jnp.where(same_head, _dot(k_end[rs, :].T, v[rs, :]), 0.0)
        st = st * dec[c * CHUNK:c * CHUNK + 8, :].T[:, 0:1] + kv
    state_ref[...] = st
    o = jnp.concatenate(heads, axis=1) + jnp.concatenate(parts, axis=0)

    ms = _dot_sel(o * o, seg) * (1.0 / HEAD_DIM)
    o = o * lax.rsqrt(ms + NORM_EPS) * ng_ref[...]
    y_ref[...] = o * (gg * _sigmoid(gg))


def _rwkv_body(first, cols_ref, vf_ref, mu_ref, w0_ref, w2_ref, a0_ref, a2_ref, g2_ref,
                 kk_ref, ka_ref, rk_ref, lg_ref, lb_ref, v0_ref, v1_ref, v2_ref,
                 y_ref, vout_ref, prev_ref, state_ref):
    L = cols_ref.shape[0]
    W = RWKV_WIDTH
    D = HEAD_DIM
    nh = W // D
    nc = L // CHUNK

    cols = cols_ref[...]
    full = jnp.concatenate([prev_ref[...], cols], axis=0)
    shifted = pltpu.roll(full, 1, axis=0)[8:, :]
    prev_ref[...] = cols[L - 8:, :]
    xx = cols + (shifted - cols) * mu_ref[...]
    r = xx[:, 0:W]
    k = xx[:, W:2 * W]
    v = xx[:, 2 * W:3 * W]
    lr = xx[:, 3 * W:]
    log_w = -_softplus(-(w0_ref[...] + _dot(jnp.tanh(lr), w2_ref[...]))) - 0.5
    lw = -jnp.exp(log_w)
    iclr = _sigmoid(a0_ref[...] + _dot(lr, a2_ref[...]))
    gate = _dot(_sigmoid(lr), g2_ref[...])
    if first:
        vout_ref[...] = v
    else:
        vout_ref[...] = jnp.zeros_like(vout_ref)
        mix = _sigmoid(v0_ref[...] + _dot(_dot(v, v1_ref[...]), v2_ref[...]))
        v = v + (vf_ref[...] - v) * mix

    yield
    ri, ci = _iota2(W, W)
    same_head = (ri // D) == (ci // D)
    seg = jnp.where(same_head, 1.0, 0.0).astype(BF16)
    kk = k * kk_ref[...]
    kk = kk / jnp.maximum(jnp.sqrt(_dot_sel(kk * kk, seg)), 1e-12)
    k = k * (1.0 + (iclr - 1.0) * ka_ref[...])
    b = kk * iclr

    _, incl, strict = _chunk_masks(L)
    g = _sel_dot(jnp.where(incl, 1.0, 0.0).astype(BF16), lw)
    gtot = jnp.concatenate(
        [jnp.broadcast_to(g[(c + 1) * CHUNK - 1:(c + 1) * CHUNK, :], (CHUNK, W)) for c in range(nc)],
        axis=0)
    e_neg = jnp.exp(-g)
    e_rem = jnp.exp(gtot - g)
    r_t = r * jnp.exp(g)
    a_t = -kk * jnp.exp(g - lw)
    b_t = b * e_neg
    k_t = k * e_neg
    b_e = b * e_rem
    k_e = k * e_rem
    w_l = jnp.exp(gtot)
    eye = jnp.where(_iota2(L, L)[0] == _iota2(L, L)[1], 1.0, 0.0)

    yield
    hsl = [slice(h * D, (h + 1) * D) for h in range(nh)]
    a_ab, a_rb, akv, arkv = [], [], [], []
    for hs in hsl:
        big = _dot_nt(jnp.concatenate([a_t[:, hs], r_t[:, hs]], axis=0),
                      jnp.concatenate([b_t[:, hs], k_t[:, hs]], axis=0))
        a_ab.append(jnp.where(strict, big[0:L, 0:L], 0.0))
        a_rb.append(jnp.where(incl, big[L:, 0:L], 0.0))
        av = _dot(jnp.concatenate([jnp.where(strict, big[0:L, L:], 0.0),
                                   jnp.where(incl, big[L:, L:], 0.0)], axis=0), v[:, hs])
        akv.append(av[0:L])
        arkv.append(av[L:])
        yield
    tinv = [eye + a for a in a_ab]
    ak = [_dot(a, a) for a in a_ab]
    yield
    for _ in range(CHUNK.bit_length() - 3):
        x = [_dot(a, jnp.concatenate([t, a], axis=1)) for t, a in zip(tinv, ak)]
        tinv = [t + xi[:, 0:L] for t, xi in zip(tinv, x)]
        ak = [xi[:, L:] for xi in x]
        yield
    tinv = [t + _dot(a, t) for t, a in zip(tinv, ak)]
    yield
    qz = [_dot(tinv[h], jnp.concatenate([a_t[:, hs], akv[h]], axis=1)) for h, hs in enumerate(hsl)]
    ry = [_dot(a_rb[h], qz[h]) for h in range(nh)]
    qa = jnp.concatenate([q[:, 0:D] for q in qz], axis=1)
    zf = jnp.concatenate([q[:, D:] for q in qz], axis=1)
    rq = r_t + jnp.concatenate([x[:, 0:D] for x in ry], axis=1)
    yf = jnp.concatenate([x[:, D:] + arkv[h] for h, x in enumerate(ry)], axis=1)

    yield
    p_c, n_c, w_c = [], [], []
    for c in range(nc):
        rs = slice(c * CHUNK, (c + 1) * CHUNK)
        b_tr = b_e[rs, :].T
        k_tr = k_e[rs, :].T
        p_c.append(jnp.where(same_head, _dot(b_tr, qa[rs, :]), 0.0))
        n_c.append(jnp.where(same_head, _dot(jnp.concatenate([b_tr, k_tr], axis=1),
                                             jnp.concatenate([zf[rs, :], v[rs, :]], axis=0)), 0.0))
        w_c.append(w_l[c * CHUNK:c * CHUNK + 8, :].T[:, 0:1])
        if c % 2 == 1:
            yield
    st = state_ref[...]
    ys = []
    for c in range(0, nc, 2):
        rs0 = slice(c * CHUNK, (c + 1) * CHUNK)
        rs1 = slice((c + 1) * CHUNK, (c + 2) * CHUNK)
        w0_row = w_l[c * CHUNK:c * CHUNK + 1, :]
        pn = _dot(p_c[c + 1], jnp.concatenate([p_c[c], n_c[c]], axis=1))
        p2 = w_c[c + 1] * p_c[c] + p_c[c + 1] * w0_row + pn[:, 0:W]
        n2 = w_c[c + 1] * n_c[c] + pn[:, W:] + n_c[c + 1]
        st1 = st * w_c[c] + _dot(p_c[c], st) + n_c[c]
        ys.append(_dot(rq[rs0, :], st) + yf[rs0, :])
        ys.append(_dot(rq[rs1, :], st1) + yf[rs1, :])
        st = st * (w_c[c] * w_c[c + 1]) + _dot(p2, st) + n2
    state_ref[...] = st
    y = jnp.concatenate(ys, axis=0)
    yield

    inv = 1.0 / D
    mean = _dot_sel(y, seg) * inv
    yc = y - mean
    var = _dot_sel(yc * yc, seg) * inv
    y = yc * lax.rsqrt(var + RWKV_LNX_EPS) * lg_ref[...] + lb_ref[...]
    y = y + _dot_sel(r * k * rk_ref[...], seg) * v
    y_ref[...] = y * gate


def _post_body(final, h_ref, ys_ref, yg_ref, yr_ref, p_ref, wo_ref, gm_ref, wu_ref, wd_ref,
                 gp_ref, wg_ref, wp_ref, gf_ref, o_ref):
    d_ff = wu_ref.shape[-1]
    d = h_ref.shape[-1]
    y = jnp.concatenate([ys_ref[...], yg_ref[...], yr_ref[...]], axis=1).astype(BF16)
    mix = jnp.dot(y, wo_ref[...], preferred_element_type=F32)
    h = h_ref[...] + mix
    hn = _rms(h, gm_ref[...]).astype(BF16)
    yield
    us = []
    for j in range(d_ff // d):
        u = jnp.dot(hn, wu_ref[:, j * d:(j + 1) * d], preferred_element_type=F32)
        us.append(jnp.square(jnp.maximum(u, 0.0)).astype(BF16))
        yield
    mlp = jnp.dot(jnp.concatenate(us, axis=1), wd_ref[...], preferred_element_type=F32)
    yield
    h = h + mlp
    gate = _sigmoid(jnp.dot(_rms(h, gp_ref[...]).astype(BF16), wg_ref[...],
                            preferred_element_type=F32))
    yield
    h = h + gate * jnp.dot(p_ref[...].astype(BF16), wp_ref[...], preferred_element_type=F32)
    if final:
        h = _rms(h, gf_ref[...])
    o_ref[...] = h


N_POST_IN = 13
N_MIX_IN = 4 + 6 + 3 + 14


def _stage_kernel(first, final, with_post, with_mix, carry, *refs):
    o = 0
    if with_post:
        post_in = refs[o:o + N_POST_IN]
        o += N_POST_IN
        if carry:
            carry_in = refs[o]
            o += 1
    if with_mix:
        h_in, g_in, w_in, vf = refs[o:o + 4]
        ssd_prm = refs[o + 4:o + 10]
        gla_prm = refs[o + 10:o + 13]
        rwkv_prm = refs[o + 13:o + N_MIX_IN]
        o += N_MIX_IN
    if with_post:
        post_out = refs[o]
        o += 1
        if carry:
            post_out[0] = carry_in[...]
            post_out = post_out.at[1]
    if with_mix:
        y_ssd, y_gla, y_rwkv, vout = refs[o:o + 4]
        scratch = refs[o + 4:]
        tail, ssd_state, gla_state, prev, rwkv_state = scratch

        @pl.when(pl.program_id(1) == 0)
        def _():
            for ref in scratch:
                ref[...] = jnp.zeros_like(ref)

    streams = []
    if with_post:
        streams.append((_post_body(final, *post_in, post_out), 1))
    if with_mix:
        def mixers():
            ssd_in, gla_in, rwkv_in, small = _in_proj(h_in, g_in, w_in)
            yield
            yield from _ssd_body(ssd_in, small, *ssd_prm, y_ssd, tail, ssd_state)
            yield from _gla_body(gla_in, small, *gla_prm, y_gla, gla_state)
            yield from _rwkv_body(first, rwkv_in, vf, *rwkv_prm, y_rwkv, vout, prev, rwkv_state)
        streams.append((mixers(), 2))
    while streams:
        for s in list(streams):
            for _ in range(s[1]):
                if next(s[0], StopIteration) is StopIteration:
                    streams.remove(s)
                    break


def _stage(nb, t, post, mix):
    L = ROW_BLOCK
    W = RWKV_WIDTH
    row = lambda bi, ti: (bi, ti, 0)
    rows = lambda a: pl.BlockSpec((None, L, a.shape[-1]), row)
    dummy = pl.BlockSpec((None, 8, SMALL), lambda bi, ti: (0, 0, 0))
    in_specs, args, out_specs, out_shape, scratch = [], [], [], [], []
    first = final = carry = False
    once = pl.Buffered(1)
    if post is not None:
        lp, hoff, poff, final = post["layer"], post["hoff"], post["poff"], post["final"]
        h, p = post["h"], post["p"]
        d = h.shape[-1]
        par = lambda bi, ti: (lp, 0, 0)
        wspec = lambda a: pl.BlockSpec((None,) + a.shape[1:], par, pipeline_mode=once)
        wo, gm, wu, wd, gp, wg, wp = post["weights"]
        in_specs += [pl.BlockSpec((None, L, d), lambda bi, ti: (bi + hoff, ti, 0)),
                     rows(post["ys"]), rows(post["yg"]), rows(post["yr"]),
                     pl.BlockSpec((None, None, L, p.shape[-1]), lambda bi, ti: (lp, bi + poff, ti, 0)),
                     wspec(wo), wspec(gm), wspec(wu), wspec(wd), wspec(gp), wspec(wg), wspec(wp),
                     pl.BlockSpec((1, d), lambda bi, ti: (0, 0))]
        args += [h, post["ys"], post["yg"], post["yr"], p, wo, gm, wu, wd, gp, wg, wp, post["gf"]]
        carry = post.get("carry") is not None
        if carry:
            in_specs.append(rows(post["carry"]))
            args.append(post["carry"])
            out_specs.append(pl.BlockSpec((2, None, L, d), lambda bi, ti: (0, bi, ti, 0)))
            out_shape.append(jax.ShapeDtypeStruct((2, nb, t, d), F32))
        else:
            out_specs.append(pl.BlockSpec((None, L, d), row))
            out_shape.append(jax.ShapeDtypeStruct((nb, t, d), F32))
    if mix is not None:
        lm = mix["layer"]
        v_first = mix["v_first"]
        first = v_first is None
        hm, hmoff = mix["h"], mix["hoff"]
        if first:
            v_first = hm
        parm = lambda bi, ti: (lm, 0, 0)
        parv = lambda bi, ti: (max(lm - 1, 0), 0, 0)
        prm = lambda a, im=parm: pl.BlockSpec((None,) + a.shape[1:], im)
        rwkv_prm = mix["rwkv_prm"]
        mix_prm = list(mix["ssd_prm"]) + list(mix["gla_prm"]) + list(rwkv_prm)
        assert 4 + len(mix_prm) == N_MIX_IN
        w_in = mix["w_in"]
        in_specs += ([pl.BlockSpec((None, L, hm.shape[-1]), lambda bi, ti: (bi + hmoff, ti, 0)),
                      prm(mix["gmix"]),
                      pl.BlockSpec((None,) + w_in.shape[1:], parm, pipeline_mode=once),
                      dummy if first else rows(v_first)]
                     + [prm(a) for a in mix_prm[:-3]] + [prm(a, parv) for a in mix_prm[-3:]])
        args += [hm, mix["gmix"], w_in, v_first] + mix_prm
        out_specs += [pl.BlockSpec((None, L, SSD_WIDTH), row), pl.BlockSpec((None, L, GLA_WIDTH), row),
                      pl.BlockSpec((None, L, W), row),
                      pl.BlockSpec((None, L, W), row) if first else dummy]
        out_shape += [jax.ShapeDtypeStruct((nb, t, SSD_WIDTH), F32),
                      jax.ShapeDtypeStruct((nb, t, GLA_WIDTH), F32),
                      jax.ShapeDtypeStruct((nb, t, W), F32),
                      jax.ShapeDtypeStruct((nb, t, W), F32) if first
                      else jax.ShapeDtypeStruct((1, 8, SMALL), F32)]
        scratch = [
            pltpu.VMEM((8, SSD_XBC), F32),
            pltpu.VMEM((SSD_GROUPS, SSD_STATE, SSD_WIDTH // SSD_GROUPS), F32),
            pltpu.VMEM((GLA_WIDTH, GLA_WIDTH), F32),
            pltpu.VMEM((8, RWKV_COLS), F32),
            pltpu.VMEM((W, W), F32),
        ]
    return pl.pallas_call(
        functools.partial(_stage_kernel, first, final, post is not None, mix is not None, carry),
        grid=(nb, t // L),
        in_specs=in_specs,
        out_specs=out_specs,
        out_shape=out_shape,
        scratch_shapes=scratch,
        compiler_params=pltpu.CompilerParams(
            dimension_semantics=("parallel", "arbitrary"), vmem_limit_bytes=VMEM_LIMIT),
        name="stage",
    )(*args)


def _rows(a):
    return a.reshape(a.shape[0], 1, -1)


def _pad_to(a, axis, n, offset=0):
    pads = [(0, 0)] * a.ndim
    pads[axis] = (offset, n - a.shape[axis] - offset)
    return jnp.pad(a, pads)


def kernel(x, p, norm_mix_g, w_in, ssd_conv_w, ssd_conv_b, ssd_dt_bias, ssd_a_log, ssd_d, ssd_norm_g, gla_alpha_w, gla_alpha_b, gla_norm_g, rwkv_mu, rwkv_w0, rwkv_w2, rwkv_a0, rwkv_a2, rwkv_g2, rwkv_k_k, rwkv_k_a, rwkv_r_k, rwkv_lnx_g, rwkv_lnx_b, rwkv_v0, rwkv_v1, rwkv_v2, w_out, norm_mlp_g, w_up, w_down, norm_ple_g, w_ple_gate, w_ple_proj, norm_final_g):
    depth = w_in.shape[0]
    assert x.shape[1] % ROW_BLOCK == 0 and ROW_BLOCK % CHUNK == 0

    ssd_cols = SSD_WIDTH + SSD_XBC + SSD_HEADS
    gla_cols = 4 * GLA_WIDTH + GLA_GATE_RANK
    s0, g0, r0 = 0, ssd_cols, ssd_cols + gla_cols
    small_w = jnp.concatenate(
        [w_in[:, :, s0 + SSD_WIDTH + SSD_XBC:g0], w_in[:, :, g0 + 4 * GLA_WIDTH:r0]], axis=-1)
    w_in_p = jnp.concatenate(
        [w_in[:, :, s0:s0 + SSD_WIDTH + SSD_XBC], w_in[:, :, g0:g0 + 4 * GLA_WIDTH],
         w_in[:, :, r0:], _pad_to(small_w, 2, SMALL)], axis=-1).astype(BF16)

    dtb = _rows(_pad_to(ssd_dt_bias, 1, SMALL))
    aneg = _rows(_pad_to(-jnp.exp(ssd_a_log), 1, SMALL))
    dskip = _rows(jnp.repeat(ssd_d, HEAD_DIM, axis=1))
    gla_aw = _pad_to(gla_alpha_w, 1, SMALL, offset=SSD_HEADS)
    gla_ng = _rows(jnp.tile(gla_norm_g, (1, GLA_WIDTH // HEAD_DIM)))
    w2p = _pad_to(rwkv_w2, 1, RWKV_LR, offset=0)
    a2p = _pad_to(rwkv_a2, 1, RWKV_LR, offset=RWKV_DECAY_RANK)
    g2p = _pad_to(rwkv_g2, 1, RWKV_LR, offset=RWKV_DECAY_RANK + RWKV_ICLR_RANK)
    v1p = _pad_to(rwkv_v1, 2, SMALL)
    v2p = _pad_to(rwkv_v2, 1, SMALL)
    rwkv_prm = (_rows(rwkv_mu), _rows(rwkv_w0), w2p, _rows(rwkv_a0), a2p, g2p,
                _rows(rwkv_k_k), _rows(rwkv_k_a), _rows(rwkv_r_k.reshape(depth, -1)),
                _rows(rwkv_lnx_g), _rows(rwkv_lnx_b), _rows(rwkv_v0), v1p, v2p)

    wo = w_out.astype(BF16)
    wu = w_up.astype(BF16)
    wd = w_down.astype(BF16)
    wg = w_ple_gate.astype(BF16)
    wp = w_ple_proj.astype(BF16)
    gmix, gmlp, gple = _rows(norm_mix_g), _rows(norm_mlp_g), _rows(norm_ple_g)
    gfin = norm_final_g.reshape(1, -1)
    cb = _rows(ssd_conv_b)
    ssd_ng = _rows(ssd_norm_g)
    gla_ab = _rows(gla_alpha_b)

    nb_all, t, _ = x.shape
    assert nb_all % 2 == 0
    nb = nb_all // 2
    weights = (wo, gmlp, wu, wd, gple, wg, wp)
    ssd_prm = (ssd_conv_w, cb, dtb, aneg, dskip, ssd_ng)
    gla_prm = (gla_aw, gla_ab, gla_ng)
    h = [x, x]
    hoff = [0, nb]
    v_first = [None, None]
    pending = None
    for i in range(depth):
        for g in (0, 1):
            mix = dict(h=h[g], hoff=hoff[g], gmix=gmix, w_in=w_in_p, v_first=v_first[g],
                       ssd_prm=ssd_prm, gla_prm=gla_prm, rwkv_prm=rwkv_prm, layer=i)
            outs = _stage(nb, t, pending, mix)
            if pending is not None:
                og = 1 - g
                h[og], hoff[og] = outs[0], 0
                outs = outs[1:]
            y_ssd, y_gla, y_rwkv, vout = outs
            if i == 0:
                v_first[g] = vout
            pending = dict(h=h[g], hoff=hoff[g], ys=y_ssd, yg=y_gla, yr=y_rwkv, p=p,
                           poff=g * nb, weights=weights, gf=gfin, layer=i, final=(i == depth - 1))
    out = _stage(nb, t, dict(pending, carry=h[0]), None)[0]
    return out.reshape(nb_all, t, -1)
```
